```python
import math
import jax, jax.numpy as jnp
from jax import lax
import numpy as np

D_MODEL = 1024
BATCH = 16
SEQ = 2048
DEPTH = 2

HEAD_DIM = 64
D_MIX = D_MODEL
POOL_WIDTH = D_MIX // 4
POOL_WINDOWS = (2, 4, 8, 16)
POOL_GROUP = POOL_WIDTH // len(POOL_WINDOWS)
MOBA_WIDTH = D_MIX // 4
MOBA_HEADS = MOBA_WIDTH // HEAD_DIM
MOBA_BLOCK = 256
MOBA_TOPK = 3
MOBA_QCHUNK = 32
DIL_WIDTH = D_MIX // 4
DIL_HEADS = DIL_WIDTH // HEAD_DIM
DILATIONS = ((128, 1), (512, 4), (2048, 16))
CONV_WIDTH = D_MIX - POOL_WIDTH - MOBA_WIDTH - DIL_WIDTH
CONV_KERNEL = 31
ROPE_THETA = 500000.0
ROPE_DIMS = HEAD_DIM // 4
D_FF = 2816
RMS_EPS = 1e-6
LN_EPS = 1e-5
NEG_INF = -1e30
OFF_POOL = 0
OFF_MOBA = OFF_POOL + POOL_WIDTH
OFF_DIL = OFF_MOBA + 3 * MOBA_WIDTH
OFF_CONV = OFF_DIL + 3 * DIL_WIDTH
D_IN = OFF_CONV + 2 * CONV_WIDTH

kernel_name = "hybrid_pool_moba_dilated_conv_macaron"


def rms_norm(x, g):
    xf = x.astype(jnp.float32)
    y = xf * lax.rsqrt(jnp.mean(xf * xf, axis=-1, keepdims=True) + RMS_EPS)
    return (y * g.astype(jnp.float32)).astype(x.dtype)


def layer_norm(x, g, b):
    xf = x.astype(jnp.float32)
    mu = jnp.mean(xf, axis=-1, keepdims=True)
    var = jnp.mean(jnp.square(xf - mu), axis=-1, keepdims=True)
    y = (xf - mu) * lax.rsqrt(var + LN_EPS)
    return (y * g.astype(jnp.float32) + b.astype(jnp.float32)).astype(x.dtype)


def swiglu(x, wg, wu, wd):
    return (jax.nn.silu(x @ wg) * (x @ wu)) @ wd


def rope_tables(positions, dtype):
    inv = ROPE_THETA ** (-jnp.arange(0, ROPE_DIMS, 2, dtype=jnp.float32) / ROPE_DIMS)
    ang = positions.astype(jnp.float32)[..., None] * inv
    return (jnp.cos(ang)[:, :, None, :].astype(dtype), jnp.sin(ang)[:, :, None, :].astype(dtype))


def apply_rope(t, cos, sin):
    half = ROPE_DIMS // 2
    t1 = t[..., :half]
    t2 = t[..., half:ROPE_DIMS]
    return jnp.concatenate([t1 * cos - t2 * sin, t2 * cos + t1 * sin, t[..., ROPE_DIMS:]], axis=-1)


def pool_mixer(u, w, scale):
    B, S, _ = u.shape
    ug = u.reshape(B, S, len(POOL_WINDOWS), POOL_GROUP)
    t_count = jnp.arange(S, dtype=jnp.float32) + 1.0
    outs = []
    for g, wnd in enumerate(POOL_WINDOWS):
        ch = ug[:, :, g].astype(jnp.float32)
        c = jnp.cumsum(ch, axis=1)
        c_back = jnp.pad(c, ((0, 0), (wnd, 0), (0, 0)))[:, :S]
        cnt = jnp.minimum(t_count, float(wnd))[None, :, None]
        outs.append(((c - c_back) / cnt - ch).astype(u.dtype))
    pooled = jnp.stack(outs, axis=2)
    mixed = jnp.einsum('bsgc,gcd->bsgd', pooled, w)
    return mixed.reshape(B, S, POOL_WIDTH) * scale


def moba_attention(q, k, v):
    B, H, S, dh = q.shape
    L = MOBA_BLOCK
    Sp = -(-S // L) * L
    pad = ((0, 0), (0, 0), (0, Sp - S), (0, 0))
    q, k, v = jnp.pad(q, pad), jnp.pad(k, pad), jnp.pad(v, pad)
    NB = Sp // L
    kb = k.reshape(B, H, NB, L, dh)
    vb = v.reshape(B, H, NB, L, dh)
    kmean = jnp.mean(kb.astype(jnp.float32), axis=3)
    gate = jnp.einsum('bhsd,bhnd->bhsn', q.astype(jnp.float32), kmean)
    qblk = jnp.arange(Sp) // L
    past = jnp.arange(NB)[None, :] < qblk[:, None]
    gate = jnp.where(past, gate, NEG_INF)
    topk = min(MOBA_TOPK, NB)
    _, idx = lax.top_k(gate, topk)
    QC = MOBA_QCHUNK
    nc = Sp // QC
    qc = q.reshape(B, H, nc, QC, dh).transpose(2, 0, 1, 3, 4)
    ic = idx.reshape(B, H, nc, QC, topk).transpose(2, 0, 1, 3, 4)
    gather = jax.vmap(jax.vmap(lambda blocks, ii: blocks[ii]))
    scale = HEAD_DIM ** -0.5

    def chunk(args):
        c, qi, ii = args
        start = c * QC
        blk = start // L
        t = start + jnp.arange(QC)
        k_own = lax.dynamic_index_in_dim(kb, blk, axis=2, keepdims=False)
        v_own = lax.dynamic_index_in_dim(vb, blk, axis=2, keepdims=False)
        s_own = jnp.einsum('bhqd,bhkd->bhqk', qi, k_own).astype(jnp.float32) * scale
        kpos = blk * L + jnp.arange(L)
        s_own = jnp.where(kpos[None, :] <= t[:, None], s_own, NEG_INF)
        k_sel = gather(kb, ii)
        v_sel = gather(vb, ii)
        s_sel = jnp.einsum('bhqd,bhqnkd->bhqnk', qi, k_sel).astype(jnp.float32) * scale
        valid = jnp.arange(topk) < blk
        s_sel = jnp.where(valid[:, None], s_sel, NEG_INF)
        logits = jnp.concatenate([s_sel.reshape(B, H, QC, topk * L), s_own], axis=-1)
        p = jax.nn.softmax(logits, axis=-1).astype(qi.dtype)
        p_sel = p[..., :topk * L].reshape(B, H, QC, topk, L)
        p_own = p[..., topk * L:]
        return (jnp.einsum('bhqnk,bhqnkd->bhqd', p_sel, v_sel)
                + jnp.einsum('bhqk,bhkd->bhqd', p_own, v_own))

    out = lax.map(chunk, (jnp.arange(nc), qc, ic))
    out = out.transpose(1, 2, 0, 3, 4).reshape(B, H, Sp, dh)
    return out[:, :, :S]


def dilated_branch(q, k, v, window, dil):
    B, H, S, dh = q.shape
    n = S // dil
    bw = window // dil
    n_p = -(-n // bw) * bw
    nb = n_p // bw

    def to_sub(t):
        t = t.reshape(B, H, n, dil, dh).transpose(0, 1, 3, 2, 4)
        t = jnp.pad(t, ((0, 0), (0, 0), (0, 0), (0, n_p - n), (0, 0)))
        return t.reshape(B, H, dil, nb, bw, dh)

    def with_prev(t):
        prev = jnp.pad(t, ((0, 0), (0, 0), (0, 0), (1, 0), (0, 0), (0, 0)))[:, :, :, :nb]
        return jnp.concatenate([prev, t], axis=4)

    qs = to_sub(q)
    kk = with_prev(to_sub(k))
    vv = with_prev(to_sub(v))
    logits = jnp.einsum('bhrnqd,bhrnkd->bhrnqk', qs, kk).astype(jnp.float32) * (HEAD_DIM ** -0.5)
    qi = jnp.arange(bw)[:, None]
    kj = jnp.arange(2 * bw)[None, :]
    rel = qi + bw - kj
    band = (rel >= 0) & (rel <= bw)
    has_prev = (jnp.arange(nb)[:, None, None] > 0) | (kj[None] >= bw)
    mask = band[None] & has_prev
    logits = jnp.where(mask, logits, NEG_INF)
    m = jnp.max(logits, axis=-1, keepdims=True)
    e = jnp.exp(logits - m)
    den = jnp.sum(e, axis=-1, keepdims=True)
    lse = (m + jnp.log(den))[..., 0]
    p = (e / den).astype(v.dtype)
    out = jnp.einsum('bhrnqk,bhrnkd->bhrnqd', p, vv)
    out = out.reshape(B, H, dil, n_p, dh)[:, :, :, :n].transpose(0, 1, 3, 2, 4).reshape(B, H, S, dh)
    lse = lse.reshape(B, H, dil, n_p)[:, :, :, :n].transpose(0, 1, 3, 2).reshape(B, H, S)
    return out, lse


def dilated_attention(q, k, v):
    outs, lses = [], []
    for window, dil in DILATIONS:
        o, l = dilated_branch(q, k, v, window, dil)
        outs.append(o)
        lses.append(l)
    w = jax.nn.softmax(jnp.stack(lses, axis=0), axis=0)
    o = jnp.stack(outs, axis=0).astype(jnp.float32)
    return jnp.sum(w[..., None] * o, axis=0).astype(q.dtype)


def conv_module(u, conv_w, conv_b, ln_g, ln_b):
    a, g = jnp.split(u, 2, axis=-1)
    h = a * jax.nn.sigmoid(g)
    y = lax.conv_general_dilated(h, conv_w[:, None, :].astype(h.dtype), window_strides=(1,),
                                 padding=((CONV_KERNEL - 1, 0),),
                                 dimension_numbers=('NWC', 'WIO', 'NWC'),
                                 feature_group_count=CONV_WIDTH)
    y = y + conv_b
    return jax.nn.silu(layer_norm(y, ln_g, ln_b))


def to_heads(t, h):
    B, S, _ = t.shape
    return t.reshape(B, S, h, HEAD_DIM)


def setup_inputs(seed: int = 0) -> dict:
    key = jax.random.key(seed)
    ks = jax.random.split(key, 24)
    f32 = jnp.float32

    def nrm(k, shape, fan_in):
        return jax.random.normal(k, shape, f32) * (fan_in ** -0.5)

    def gain(k, shape):
        return 1.0 + 0.05 * jax.random.normal(k, shape, f32)

    x = jax.random.normal(ks[0], (BATCH, SEQ, D_MODEL), f32)
    positions = jnp.arange(SEQ, dtype=jnp.int32)[None, :] + jax.random.randint(ks[1], (BATCH, 1), 0, 4096, jnp.int32)
    return {
        "x": x,
        "positions": positions,
        "ffn1_norm": gain(ks[2], (DEPTH, D_MODEL)),
        "ffn1_gate": nrm(ks[3], (DEPTH, D_MODEL, D_FF), D_MODEL),
        "ffn1_up": nrm(ks[4], (DEPTH, D_MODEL, D_FF), D_MODEL),
        "ffn1_down": nrm(ks[5], (DEPTH, D_FF, D_MODEL), D_FF),
        "mix_norm": gain(ks[6], (DEPTH, D_MODEL)),
        "w_in": nrm(ks[7], (DEPTH, D_MODEL, D_IN), D_MODEL),
        "pool_w": nrm(ks[8], (DEPTH, len(POOL_WINDOWS), POOL_GROUP, POOL_GROUP), POOL_GROUP),
        "pool_scale": 1.0 + 0.1 * jax.random.normal(ks[9], (DEPTH, POOL_WIDTH), f32),
        "conv_w": nrm(ks[10], (DEPTH, CONV_KERNEL, CONV_WIDTH), CONV_KERNEL),
        "conv_b": 0.01 * jax.random.normal(ks[11], (DEPTH, CONV_WIDTH), f32),
        "conv_ln_g": gain(ks[12], (DEPTH, CONV_WIDTH)),
        "conv_ln_b": 0.01 * jax.random.normal(ks[13], (DEPTH, CONV_WIDTH), f32),
        "w_out": nrm(ks[14], (DEPTH, D_MIX, D_MODEL), D_MIX),
        "ffn2_norm": gain(ks[15], (DEPTH, D_MODEL)),
        "ffn2_gate": nrm(ks[16], (DEPTH, D_MODEL, D_FF), D_MODEL),
        "ffn2_up": nrm(ks[17], (DEPTH, D_MODEL, D_FF), D_MODEL),
        "ffn2_down": nrm(ks[18], (DEPTH, D_FF, D_MODEL), D_FF),
        "final_norm": gain(ks[19], (D_MODEL,)),
    }


def reference(x, positions, ffn1_norm, ffn1_gate, ffn1_up, ffn1_down, mix_norm, w_in, pool_w, pool_scale,
              conv_w, conv_b, conv_ln_g, conv_ln_b, w_out, ffn2_norm, ffn2_gate, ffn2_up, ffn2_down, final_norm):
    cos, sin = rope_tables(positions, x.dtype)
    for l in range(DEPTH):
        x = x + 0.5 * swiglu(rms_norm(x, ffn1_norm[l]), ffn1_gate[l], ffn1_up[l], ffn1_down[l])
        h = rms_norm(x, mix_norm[l]) @ w_in[l]
        u_pool = h[..., OFF_POOL:OFF_MOBA]
        qm, km, vm = jnp.split(h[..., OFF_MOBA:OFF_DIL], 3, axis=-1)
        qd, kd, vd = jnp.split(h[..., OFF_DIL:OFF_CONV], 3, axis=-1)
        u_conv = h[..., OFF_CONV:]
        B, S, _ = x.shape
        y_pool = pool_mixer(u_pool, pool_w[l], pool_scale[l])
        qm = apply_rope(to_heads(qm, MOBA_HEADS), cos, sin).transpose(0, 2, 1, 3)
        km = apply_rope(to_heads(km, MOBA_HEADS), cos, sin).transpose(0, 2, 1, 3)
        vm = to_heads(vm, MOBA_HEADS).transpose(0, 2, 1, 3)
        y_moba = moba_attention(qm, km, vm).transpose(0, 2, 1, 3).reshape(B, S, MOBA_WIDTH)
        qd = apply_rope(to_heads(qd, DIL_HEADS), cos, sin).transpose(0, 2, 1, 3)
        kd = apply_rope(to_heads(kd, DIL_HEADS), cos, sin).transpose(0, 2, 1, 3)
        vd = to_heads(vd, DIL_HEADS).transpose(0, 2, 1, 3)
        y_dil = dilated_attention(qd, kd, vd).transpose(0, 2, 1, 3).reshape(B, S, DIL_WIDTH)
        y_conv = conv_module(u_conv, conv_w[l], conv_b[l], conv_ln_g[l], conv_ln_b[l])
        mix = jnp.concatenate([y_pool, y_moba, y_dil, y_conv], axis=-1)
        x = x + mix @ w_out[l]
        x = x + 0.5 * swiglu(rms_norm(x, ffn2_norm[l]), ffn2_gate[l], ffn2_up[l], ffn2_down[l])
    return rms_norm(x, final_norm)
```

```python
import functools

import numpy as np
import jax
import jax.numpy as jnp
from jax import lax
from jax.experimental import pallas as pl
from jax.experimental.pallas import tpu as pltpu

F32 = jnp.float32
BF16 = jnp.bfloat16

D_MODEL = 1024
SEQ = 2048
DEPTH = 2
HEAD_DIM = 64
N_HEADS = 4
GROUP_W = 256
POOL_WINDOWS = (2, 4, 8, 16)
POOL_GROUP = 64
MOBA_BLOCK = 256
MOBA_NB = SEQ // MOBA_BLOCK
MOBA_TOPK = 3
CONV_KERNEL = 31
ROPE_THETA = 500000.0
ROPE_DIMS = 16
D_FF = 2816
D_IN = 2304
RMS_EPS = 1e-6
LN_EPS = 1e-5
NEG_INF = -1e30
ATTN_SCALE = HEAD_DIM ** -0.5
Q_BLOCK = 256
CONV_PAD = 32
CONV_ROWS = 128

VMEM_LIMIT = 56 * 1024 * 1024


def _cparams(*sem):
    return pltpu.CompilerParams(dimension_semantics=sem, vmem_limit_bytes=VMEM_LIMIT)


def _rms(x, g):
    return x * lax.rsqrt(jnp.mean(x * x, axis=-1, keepdims=True) + RMS_EPS) * g


def _ffn_kernel(x_ref, g_ref, wg_ref, wu_ref, wd_ref, fin_ref, o_ref, xn_ref, acc_ref, *, n_ff, final_norm):
    j = pl.program_id(1)

    @pl.when(j == 0)
    def _():
        xn_ref[...] = _rms(x_ref[...], g_ref[...]).astype(BF16)

    xn = xn_ref[...]
    gate = jnp.dot(xn, wg_ref[...], preferred_element_type=F32)
    up = jnp.dot(xn, wu_ref[...], preferred_element_type=F32)
    hid = (jax.nn.silu(gate) * up).astype(BF16)
    down = jnp.dot(hid, wd_ref[...], preferred_element_type=F32)

    @pl.when(j == 0)
    def _():
        acc_ref[...] = down

    @pl.when(j > 0)
    def _():
        acc_ref[...] += down

    @pl.when(j == n_ff - 1)
    def _():
        y = x_ref[...] + 0.5 * acc_ref[...]
        if final_norm:
            y = _rms(y, fin_ref[...])
        o_ref[...] = y


def _ffn(x2d, norm_g, wg, wu, wd, fin_g, *, final_norm, tm=512, tf=1408):
    n_tok = x2d.shape[0]
    n_ff = D_FF // tf
    return pl.pallas_call(
        functools.partial(_ffn_kernel, n_ff=n_ff, final_norm=final_norm),
        grid=(n_tok // tm, n_ff),
        in_specs=[
            pl.BlockSpec((tm, D_MODEL), lambda i, j: (i, 0)),
            pl.BlockSpec((1, D_MODEL), lambda i, j: (0, 0)),
            pl.BlockSpec((D_MODEL, tf), lambda i, j: (0, j)),
            pl.BlockSpec((D_MODEL, tf), lambda i, j: (0, j)),
            pl.BlockSpec((tf, D_MODEL), lambda i, j: (j, 0)),
            pl.BlockSpec((1, D_MODEL), lambda i, j: (0, 0)),
        ],
        out_specs=pl.BlockSpec((tm, D_MODEL), lambda i, j: (i, 0)),
        out_shape=jax.ShapeDtypeStruct((n_tok, D_MODEL), F32),
        scratch_shapes=[pltpu.VMEM((tm, D_MODEL), BF16), pltpu.VMEM((tm, D_MODEL), F32)],
        compiler_params=_cparams("parallel", "arbitrary"),
        name="ffn",
    )(x2d, norm_g, wg, wu, wd, fin_g)


def _rope(t, c, s_up, s_dn):
    half = ROPE_DIMS // 2
    return t * c + pltpu.roll(t, half, axis=1) * s_up + pltpu.roll(t, 128 - half, axis=1) * s_dn


def _inproj_kernel(x_ref, g_ref, w_ref, c_ref, su_ref, sd_ref, pool_ref, moba_ref, dil_ref, conv_ref):
    xn = _rms(x_ref[0], g_ref[...]).astype(BF16)
    h = jnp.dot(xn, w_ref[...], preferred_element_type=F32)
    c, su, sd = c_ref[0], su_ref[0], sd_ref[0]
    pool_ref[0] = h[:, 0:GROUP_W]
    for out_ref, off in ((moba_ref, GROUP_W), (dil_ref, 4 * GROUP_W)):
        for part in range(6):
            piece = h[:, off + part * 128: off + (part + 1) * 128]
            if part < 4:
                piece = _rope(piece, c, su, sd)
            out_ref[0, :, part * 128:(part + 1) * 128] = piece
    conv_ref[0] = h[:, 7 * GROUP_W:]


def _inproj(x, norm_g, w_in, rope_c, rope_su, rope_sd, *, tm=512):
    B = x.shape[0]
    tok = lambda b, i: (b, i, 0)
    const = lambda b, i: (0, 0)
    return pl.pallas_call(
        _inproj_kernel,
        grid=(B, SEQ // tm),
        in_specs=[
            pl.BlockSpec((1, tm, D_MODEL), tok),
            pl.BlockSpec((1, D_MODEL), const),
            pl.BlockSpec((D_MODEL, D_IN), const),
            pl.BlockSpec((1, tm, 128), tok),
            pl.BlockSpec((1, tm, 128), tok),
            pl.BlockSpec((1, tm, 128), tok),
        ],
        out_specs=[
            pl.BlockSpec((1, tm, GROUP_W), tok),
            pl.BlockSpec((1, tm, 3 * GROUP_W), tok),
            pl.BlockSpec((1, tm, 3 * GROUP_W), tok),
            pl.BlockSpec((1, tm, 2 * GROUP_W), tok),
        ],
        out_shape=[
            jax.ShapeDtypeStruct((B, SEQ, GROUP_W), F32),
            jax.ShapeDtypeStruct((B, SEQ, 3 * GROUP_W), F32),
            jax.ShapeDtypeStruct((B, SEQ, 3 * GROUP_W), F32),
            jax.ShapeDtypeStruct((B, SEQ, 2 * GROUP_W), F32),
        ],
        compiler_params=_cparams("parallel", "parallel"),
        name="inproj",
    )(x, norm_g, w_in, rope_c, rope_su, rope_sd)


def _pool_kernel(u_ref, w_ref, scale_ref, o_ref):
    u = u_ref[0]
    row = lax.broadcasted_iota(jnp.int32, u.shape, 0)
    lane = lax.broadcasted_iota(jnp.int32, u.shape, 1)

    def shifted(x, s):
        return jnp.where(row >= s, pltpu.roll(x, s, axis=0), 0.0)

    s2 = u + shifted(u, 1)
    s4 = s2 + shifted(s2, 2)
    s8 = s4 + shifted(s4, 4)
    s16 = s8 + shifted(s8, 8)
    grp = lane // POOL_GROUP
    wsum = jnp.where(grp == 0, s2, jnp.where(grp == 1, s4, jnp.where(grp == 2, s8, s16)))
    wnd = jnp.where(grp == 0, 2, jnp.where(grp == 1, 4, jnp.where(grp == 2, 8, 16)))
    cnt = jnp.minimum(row + 1, wnd).astype(F32)
    pooled = (wsum / cnt - u).astype(BF16)
    mixed = jnp.dot(pooled, w_ref[...], preferred_element_type=F32)
    o_ref[0] = (mixed * scale_ref[...]).astype(BF16)


def _pool(u, w_blockdiag, scale):
    B = u.shape[0]
    return pl.pallas_call(
        _pool_kernel,
        grid=(B,),
        in_specs=[
            pl.BlockSpec((1, SEQ, GROUP_W), lambda b: (b, 0, 0)),
            pl.BlockSpec((GROUP_W, GROUP_W), lambda b: (0, 0)),
            pl.BlockSpec((1, GROUP_W), lambda b: (0, 0)),
        ],
        out_specs=pl.BlockSpec((1, SEQ, GROUP_W), lambda b: (b, 0, 0)),
        out_shape=jax.ShapeDtypeStruct((B, SEQ, GROUP_W), BF16),
        compiler_params=_cparams("parallel"),
        name="pool",
    )(u, w_blockdiag, scale)


def _conv_kernel(u_ref, w_ref, b_ref, lg_ref, lb_ref, o_ref, hp_ref):
    a = u_ref[0, :, 0:GROUP_W]
    g = u_ref[0, :, GROUP_W:]
    hp_ref[0:CONV_PAD, :] = jnp.zeros((CONV_PAD, GROUP_W), F32)
    hp_ref[CONV_PAD:, :] = a * jax.nn.sigmoid(g)
    first = CONV_PAD - (CONV_KERNEL - 1)
    for c in range(SEQ // CONV_ROWS):
        r0 = c * CONV_ROWS
        acc = w_ref[0:1, :] * hp_ref[r0 + first: r0 + first + CONV_ROWS, :]
        for j in range(1, CONV_KERNEL):
            acc = acc + w_ref[j:j + 1, :] * hp_ref[r0 + first + j: r0 + first + j + CONV_ROWS, :]
        y = acc + b_ref[...]
        mu = jnp.mean(y, axis=-1, keepdims=True)
        var = jnp.mean(jnp.square(y - mu), axis=-1, keepdims=True)
        z = (y - mu) * lax.rsqrt(var + LN_EPS) * lg_ref[...] + lb_ref[...]
        o_ref[0, r0:r0 + CONV_ROWS, :] = jax.nn.silu(z).astype(BF16)


def _conv(u, w, b, ln_g, ln_b):
    B = u.shape[0]
    vec = pl.BlockSpec((1, GROUP_W), lambda b_: (0, 0))
    return pl.pallas_call(
        _conv_kernel,
        grid=(B,),
        in_specs=[
            pl.BlockSpec((1, SEQ, 2 * GROUP_W), lambda b_: (b_, 0, 0)),
            pl.BlockSpec((CONV_KERNEL, GROUP_W), lambda b_: (0, 0)),
            vec, vec, vec,
        ],
        out_specs=pl.BlockSpec((1, SEQ, GROUP_W), lambda b_: (b_, 0, 0)),
        out_shape=jax.ShapeDtypeStruct((B, SEQ, GROUP_W), BF16),
        scratch_shapes=[pltpu.VMEM((SEQ + CONV_PAD, GROUP_W), F32)],
        compiler_params=_cparams("parallel"),
        name="conv",
    )(u, w, b, ln_g, ln_b)


def _softmax_pv(s_parts, v_parts):
    m = s_parts[0].max(axis=-1, keepdims=True)
    for s in s_parts[1:]:
        m = jnp.maximum(m, s.max(axis=-1, keepdims=True))
    den = None
    out = None
    for s, v in zip(s_parts, v_parts):
        p = jnp.exp(s - m)
        ps = p.sum(axis=-1, keepdims=True)
        pv = jnp.dot(p.astype(BF16), v, preferred_element_type=F32)
        den = ps if den is None else den + ps
        out = pv if out is None else out + pv
    return out / den


def _nt_dot(a, b):
    return lax.dot_general(a, b, (((1,), (1,)), ((), ())), preferred_element_type=F32)


def _moba_kernel(qkv_ref, causal_ref, o_ref, qa_ref, ka_ref, v_ref):
    S = SEQ
    blk_row = lax.broadcasted_iota(jnp.int32, (MOBA_NB, S), 0)
    q_blk = lax.broadcasted_iota(jnp.int32, (MOBA_NB, S), 1) // MOBA_BLOCK
    past = blk_row < q_blk
    key_blk = lax.broadcasted_iota(jnp.int32, (S, HEAD_DIM), 0) // MOBA_BLOCK
    onehot = (lax.broadcasted_iota(jnp.int32, (S, HEAD_DIM), 1) == key_blk).astype(F32)

    for h in range(N_HEADS):
        q = qkv_ref[0, :, h * HEAD_DIM:(h + 1) * HEAD_DIM]
        k = qkv_ref[0, :, GROUP_W + h * HEAD_DIM: GROUP_W + (h + 1) * HEAD_DIM]
        v = qkv_ref[0, :, 2 * GROUP_W + h * HEAD_DIM: 2 * GROUP_W + (h + 1) * HEAD_DIM]
        kmean = jnp.mean(k.reshape(MOBA_NB, MOBA_BLOCK, HEAD_DIM), axis=1)
        gate = lax.dot_general(kmean, q, (((1,), (1,)), ((), ())),
                               precision=lax.Precision.HIGHEST, preferred_element_type=F32)
        gate = jnp.where(past, gate, NEG_INF)
        rank = jnp.zeros((MOBA_NB, S), jnp.int32)
        for jp in range(MOBA_NB):
            gj = gate[jp:jp + 1, :]
            beats = (gj > gate) | ((gj == gate) & (jp < blk_row))
            rank = rank + beats.astype(jnp.int32)
        keep = (past & (rank < MOBA_TOPK)) | (blk_row == q_blk)
        bias_t = jnp.where(keep, 0.0, NEG_INF)
        bias_t = jnp.concatenate([bias_t, jnp.zeros((128 - MOBA_NB, S), F32)], axis=0)
        bias = bias_t.T
        qa_ref[h] = jnp.concatenate([q * ATTN_SCALE, bias[:, 0:HEAD_DIM]], axis=1).astype(BF16)
        ka_ref[h] = jnp.concatenate([k, onehot], axis=1).astype(BF16)
        v_ref[h] = v.astype(BF16)

    causal = causal_ref[...]
    for i in range(MOBA_NB):
        r0 = i * Q_BLOCK
        outs = []
        for h in range(N_HEADS):
            qa = qa_ref[h, r0:r0 + Q_BLOCK, :]
            s_own = _nt_dot(qa, ka_ref[h, r0:r0 + Q_BLOCK, :]) + causal
            s_parts, v_parts = [s_own], [v_ref[h, r0:r0 + Q_BLOCK, :]]
            if i > 0:
                s_parts.append(_nt_dot(qa, ka_ref[h, 0:r0, :]))
                v_parts.append(v_ref[h, 0:r0, :])
            outs.append(_softmax_pv(s_parts, v_parts))
        o_ref[0, r0:r0 + Q_BLOCK, :] = jnp.concatenate(outs, axis=1).astype(BF16)


def _moba(qkv, causal_bias):
    B = qkv.shape[0]
    return pl.pallas_call(
        _moba_kernel,
        grid=(B,),
        in_specs=[
            pl.BlockSpec((1, SEQ, 3 * GROUP_W), lambda b: (b, 0, 0)),
            pl.BlockSpec((Q_BLOCK, Q_BLOCK), lambda b: (0, 0)),
        ],
        out_specs=pl.BlockSpec((1, SEQ, GROUP_W), lambda b: (b, 0, 0)),
        out_shape=jax.ShapeDtypeStruct((B, SEQ, GROUP_W), BF16),
        scratch_shapes=[
            pltpu.VMEM((N_HEADS, SEQ, 2 * HEAD_DIM), BF16),
            pltpu.VMEM((N_HEADS, SEQ, 2 * HEAD_DIM), BF16),
            pltpu.VMEM((N_HEADS, SEQ, HEAD_DIM), BF16),
        ],
        compiler_params=_cparams("parallel"),
        name="moba",
    )(qkv, causal_bias)


def _dil_kernel(qkv_ref, bias_ref, o_ref, q_ref, k_ref, v_ref):
    for h in range(N_HEADS):
        q_ref[h] = (qkv_ref[0, :, h * HEAD_DIM:(h + 1) * HEAD_DIM] * ATTN_SCALE).astype(BF16)
        k_ref[h] = qkv_ref[0, :, GROUP_W + h * HEAD_DIM: GROUP_W + (h + 1) * HEAD_DIM].astype(BF16)
        v_ref[h] = qkv_ref[0, :, 2 * GROUP_W + h * HEAD_DIM: 2 * GROUP_W + (h + 1) * HEAD_DIM].astype(BF16)
    for i in range(SEQ // Q_BLOCK):
        r0 = i * Q_BLOCK
        n = r0 + Q_BLOCK
        bias = bias_ref[:, SEQ - n:]
        outs = []
        for h in range(N_HEADS):
            s = _nt_dot(q_ref[h, r0:n, :], k_ref[h, 0:n, :]) + bias
            outs.append(_softmax_pv([s], [v_ref[h, 0:n, :]]))
        o_ref[0, r0:n, :] = jnp.concatenate(outs, axis=1).astype(BF16)


def _dil(qkv, dist_bias):
    B = qkv.shape[0]
    return pl.pallas_call(
        _dil_kernel,
        grid=(B,),
        in_specs=[
            pl.BlockSpec((1, SEQ, 3 * GROUP_W), lambda b: (b, 0, 0)),
            pl.BlockSpec((Q_BLOCK, SEQ), lambda b: (0, 0)),
        ],
        out_specs=pl.BlockSpec((1, SEQ, GROUP_W), lambda b: (b, 0, 0)),
        out_shape=jax.ShapeDtypeStruct((B, SEQ, GROUP_W), BF16),
        scratch_shapes=[pltpu.VMEM((N_HEADS, SEQ, HEAD_DIM), BF16)] * 3,
        compiler_params=_cparams("parallel"),
        name="dilated",
    )(qkv, dist_bias)


def _outproj_kernel(x_ref, yp_ref, ym_ref, yd_ref, yc_ref, w_ref, o_ref):
    mix = jnp.concatenate([yp_ref[...], ym_ref[...], yd_ref[...], yc_ref[...]], axis=1)
    o_ref[...] = x_ref[...] + jnp.dot(mix, w_ref[...], preferred_element_type=F32)


def _outproj(x2d, yp, ym, yd, yc, w_out, *, tm=512):
    n_tok = x2d.shape[0]
    grp = pl.BlockSpec((tm, GROUP_W), lambda i: (i, 0))
    return pl.pallas_call(
        _outproj_kernel,
        grid=(n_tok // tm,),
        in_specs=[pl.BlockSpec((tm, D_MODEL), lambda i: (i, 0)), grp, grp, grp, grp,
                  pl.BlockSpec((D_MODEL, D_MODEL), lambda i: (0, 0))],
        out_specs=pl.BlockSpec((tm, D_MODEL), lambda i: (i, 0)),
        out_shape=jax.ShapeDtypeStruct((n_tok, D_MODEL), F32),
        compiler_params=_cparams("parallel"),
        name="outproj",
    )(x2d, yp, ym, yd, yc, w_out)


def _rope_tables(positions):
    half = ROPE_DIMS // 2
    inv = ROPE_THETA ** (-jnp.arange(0, ROPE_DIMS, 2, dtype=F32) / ROPE_DIMS)
    ang = positions.astype(F32)[..., None] * inv
    cos, sin = jnp.cos(ang), jnp.sin(ang)
    zeros = lambda n: jnp.zeros(ang.shape[:-1] + (n,), F32)
    ones = lambda n: jnp.ones(ang.shape[:-1] + (n,), F32)
    c = jnp.concatenate([cos, cos, ones(HEAD_DIM - ROPE_DIMS)], axis=-1)
    s_up = jnp.concatenate([zeros(half), sin, zeros(HEAD_DIM - ROPE_DIMS)], axis=-1)
    s_dn = jnp.concatenate([-sin, zeros(HEAD_DIM - half)], axis=-1)
    two = lambda t: jnp.concatenate([t, t], axis=-1)
    return two(c), two(s_up), two(s_dn)


def _causal_bias():
    r = np.arange(Q_BLOCK)
    return jnp.asarray(np.where(r[None, :] <= r[:, None], 0.0, NEG_INF), F32)


def _dilated_distance_bias():
    r = np.arange(Q_BLOCK)[:, None]
    x = np.arange(SEQ)[None, :]
    d = r - (x - (SEQ - Q_BLOCK))
    cnt = ((d >= 0) & (d <= 128)).astype(np.int64)
    cnt = cnt + ((d >= 0) & (d % 4 == 0) & (d <= 512))
    cnt = cnt + ((d >= 0) & (d % 16 == 0) & (d <= 2048))
    return jnp.asarray(np.where(cnt > 0, np.log(np.maximum(cnt, 1)), NEG_INF), F32)


def _pool_blockdiag(pool_w):
    n = len(POOL_WINDOWS)
    eye = jnp.eye(n, dtype=pool_w.dtype)
    return jnp.einsum('gcd,gh->gchd', pool_w, eye).reshape(n * POOL_GROUP, n * POOL_GROUP)


def kernel(x, positions, ffn1_norm, ffn1_gate, ffn1_up, ffn1_down, mix_norm, w_in, pool_w, pool_scale,
           conv_w, conv_b, conv_ln_g, conv_ln_b, w_out, ffn2_norm, ffn2_gate, ffn2_up, ffn2_down, final_norm):
    B, S, D = x.shape
    assert (S, D) == (SEQ, D_MODEL)
    rope_c, rope_su, rope_sd = _rope_tables(positions)
    causal = _causal_bias()
    dist_bias = _dilated_distance_bias()
    fin = final_norm.reshape(1, D)
    row = lambda t: t.reshape(1, -1)
    x2d = x.reshape(B * S, D)
    for l in range(DEPTH):
        x2d = _ffn(x2d, row(ffn1_norm[l]), ffn1_gate[l].astype(BF16), ffn1_up[l].astype(BF16),
                   ffn1_down[l].astype(BF16), fin, final_norm=False)
        u_pool, qkv_m, qkv_d, u_conv = _inproj(x2d.reshape(B, S, D), row(mix_norm[l]), w_in[l].astype(BF16),
                                               rope_c, rope_su, rope_sd)
        y_pool = _pool(u_pool, _pool_blockdiag(pool_w[l]).astype(BF16), row(pool_scale[l]))
        y_moba = _moba(qkv_m, causal)
        y_dil = _dil(qkv_d, dist_bias)
        y_conv = _conv(u_conv, conv_w[l], row(conv_b[l]), row(conv_ln_g[l]), row(conv_ln_b[l]))
        flat = lambda t: t.reshape(B * S, GROUP_W)
        x2d = _outproj(x2d, flat(y_pool), flat(y_moba), flat(y_dil), flat(y_conv), w_out[l].astype(BF16))
        x2d = _ffn(x2d, row(ffn2_norm[l]), ffn2_gate[l].astype(BF16), ffn2_up[l].astype(BF16),
                   ffn2_down[l].astype(BF16), fin, final_norm=(l == DEPTH - 1))
    return x2d.reshape(B, S, D)
```

```python
import functools
import math

import numpy as np
import jax
import jax.numpy as jnp
from jax import lax
from jax.experimental import pallas as pl
from jax.experimental.pallas import tpu as pltpu

F32 = jnp.float32
BF16 = jnp.bfloat16

D_MODEL = 1024
SEQ = 2048
DEPTH = 2
HEAD_DIM = 64
N_HEADS = 4
GROUP_W = 256
POOL_WINDOWS = (2, 4, 8, 16)
POOL_GROUP = 64
MOBA_BLOCK = 256
MOBA_NB = SEQ // MOBA_BLOCK
MOBA_TOPK = 3
CONV_KERNEL = 31
ROPE_THETA = 500000.0
ROPE_DIMS = 16
D_FF = 2816
FF_CHUNK = 1408
D_IN = 2304
RMS_EPS = 1e-6
LN_EPS = 1e-5
NEG_INF = -1e30
ATTN_SCALE = HEAD_DIM ** -0.5
LOG2E = math.log2(math.e)
Q_SCALE = ATTN_SCALE * LOG2E
Q_BLOCK = 256
KEY_CHUNK = 256
Q_CHUNKS = Q_BLOCK // KEY_CHUNK
LANES = 128
SUBLANES = 8
BF16_ROWS = 16
VT_ROWS = HEAD_DIM + BF16_ROWS
CONV_PAD = 32
CONV_COPY_ROWS = SEQ + CONV_PAD - SUBLANES
CONV_ROWS = 128

VMEM_LIMIT = 56 * 1024 * 1024


def _cparams(*sem):
    return pltpu.CompilerParams(dimension_semantics=sem, vmem_limit_bytes=VMEM_LIMIT)


def _rms(x, g):
    return x * lax.rsqrt(jnp.mean(x * x, axis=-1, keepdims=True) + RMS_EPS) * g


def _resident(shape):
    return pl.BlockSpec(shape, lambda *_: (0,) * len(shape), pipeline_mode=pl.Buffered(1))


def _ffn_kernel(x_ref, g_ref, wg_ref, wu_ref, wd_ref, fin_ref, o_ref, hid_ref, *, final_norm):
    x = x_ref[...]
    xn = _rms(x, g_ref[...]).astype(BF16)
    for c in range(D_FF // FF_CHUNK):
        cols = slice(c * FF_CHUNK, (c + 1) * FF_CHUNK)
        gate = jnp.dot(xn, wg_ref[:, cols], preferred_element_type=F32)
        up = jnp.dot(xn, wu_ref[:, cols], preferred_element_type=F32)
        hid_ref[:, cols] = (jax.nn.silu(gate) * up).astype(BF16)
    y = x + 0.5 * jnp.dot(hid_ref[...], wd_ref[...], preferred_element_type=F32)
    if final_norm:
        y = _rms(y, fin_ref[...])
    o_ref[...] = y


def _ffn(x2d, norm_g, wg, wu, wd, fin_g, *, final_norm, tm=512):
    n_tok = x2d.shape[0]
    return pl.pallas_call(
        functools.partial(_ffn_kernel, final_norm=final_norm),
        grid=(n_tok // tm,),
        in_specs=[
            pl.BlockSpec((tm, D_MODEL), lambda i: (i, 0)),
            _resident((1, D_MODEL)),
            _resident((D_MODEL, D_FF)),
            _resident((D_MODEL, D_FF)),
            _resident((D_FF, D_MODEL)),
            _resident((1, D_MODEL)),
        ],
        out_specs=pl.BlockSpec((tm, D_MODEL), lambda i: (i, 0)),
        out_shape=jax.ShapeDtypeStruct((n_tok, D_MODEL), F32),
        scratch_shapes=[pltpu.VMEM((tm, D_FF), BF16)],
        compiler_params=_cparams("parallel"),
        name="ffn",
    )(x2d, norm_g, wg, wu, wd, fin_g)


def _rope_table_kernel(pos_ref, inv_ref, c_ref, su_ref, sd_ref):
    half = ROPE_DIMS // 2
    ang = pos_ref[0].astype(F32) * inv_ref[...]
    cos, sin = jnp.cos(ang), jnp.sin(ang)
    zeros = lambda n: jnp.zeros((n, SEQ), F32)
    rest = HEAD_DIM - ROPE_DIMS
    c = jnp.concatenate([cos, cos, jnp.ones((rest, SEQ), F32)] * 2, axis=0)
    s_up = jnp.concatenate([zeros(half), sin, zeros(rest)] * 2, axis=0)
    s_dn = jnp.concatenate([-sin, zeros(half), zeros(rest)] * 2, axis=0)
    c_ref[0] = c.T
    su_ref[0] = s_up.T
    sd_ref[0] = s_dn.T


def _rope_tables(positions):
    B = positions.shape[0]
    half = ROPE_DIMS // 2
    inv = ROPE_THETA ** (-jnp.arange(0, ROPE_DIMS, 2, dtype=F32) / ROPE_DIMS)
    table = jax.ShapeDtypeStruct((B, SEQ, LANES), F32)
    out_spec = pl.BlockSpec((1, SEQ, LANES), lambda b: (b, 0, 0))
    return pl.pallas_call(
        _rope_table_kernel,
        grid=(B,),
        in_specs=[pl.BlockSpec((1, 1, SEQ), lambda b: (b, 0, 0)), _resident((half, 1))],
        out_specs=[out_spec] * 3,
        out_shape=[table] * 3,
        compiler_params=_cparams("parallel"),
        name="rope_tables",
    )(positions.reshape(B, 1, SEQ), inv.reshape(half, 1))


def _rope(t, c, s_up, s_dn):
    half = ROPE_DIMS // 2
    return t * c + pltpu.roll(t, half, axis=1) * s_up + pltpu.roll(t, LANES - half, axis=1) * s_dn


def _inproj_kernel(x_ref, g_ref, w_ref, c_ref, su_ref, sd_ref, pool_ref, moba_ref, dil_ref, conv_ref):
    xn = _rms(x_ref[0], g_ref[...]).astype(BF16)
    h = jnp.dot(xn, w_ref[...], preferred_element_type=F32)
    c, su, sd = c_ref[0], su_ref[0], sd_ref[0]
    pool_ref[0] = h[:, 0:GROUP_W]
    for out_ref, off in ((moba_ref, GROUP_W), (dil_ref, 4 * GROUP_W)):
        for part in range(6):
            piece = h[:, off + part * LANES: off + (part + 1) * LANES]
            if part < 4:
                piece = _rope(piece, c, su, sd)
            out_ref[0, :, part * LANES:(part + 1) * LANES] = piece
    conv_ref[0] = h[:, 7 * GROUP_W:]


def _inproj(x, norm_g, w_in, rope_c, rope_su, rope_sd, *, tm=512):
    B = x.shape[0]
    tok = lambda b, i: (b, i, 0)
    return pl.pallas_call(
        _inproj_kernel,
        grid=(B, SEQ // tm),
        in_specs=[
            pl.BlockSpec((1, tm, D_MODEL), tok),
            _resident((1, D_MODEL)),
            _resident((D_MODEL, D_IN)),
            pl.BlockSpec((1, tm, LANES), tok),
            pl.BlockSpec((1, tm, LANES), tok),
            pl.BlockSpec((1, tm, LANES), tok),
        ],
        out_specs=[
            pl.BlockSpec((1, tm, GROUP_W), tok),
            pl.BlockSpec((1, tm, 3 * GROUP_W), tok),
            pl.BlockSpec((1, tm, 3 * GROUP_W), tok),
            pl.BlockSpec((1, tm, 2 * GROUP_W), tok),
        ],
        out_shape=[
            jax.ShapeDtypeStruct((B, SEQ, GROUP_W), F32),
            jax.ShapeDtypeStruct((B, SEQ, 3 * GROUP_W), F32),
            jax.ShapeDtypeStruct((B, SEQ, 3 * GROUP_W), F32),
            jax.ShapeDtypeStruct((B, SEQ, 2 * GROUP_W), F32),
        ],
        compiler_params=_cparams("parallel", "parallel"),
        name="inproj",
    )(x, norm_g, w_in, rope_c, rope_su, rope_sd)


def _pool_kernel(u_ref, w_ref, scale_ref, o_ref):
    u = u_ref[0]
    row = lax.broadcasted_iota(jnp.int32, u.shape, 0)
    lane = lax.broadcasted_iota(jnp.int32, u.shape, 1)

    def shifted(x, s):
        return jnp.where(row >= s, pltpu.roll(x, s, axis=0), 0.0)

    s2 = u + shifted(u, 1)
    s4 = s2 + shifted(s2, 2)
    s8 = s4 + shifted(s4, 4)
    s16 = s8 + shifted(s8, 8)
    grp = lane // POOL_GROUP
    wsum = jnp.where(grp == 0, s2, jnp.where(grp == 1, s4, jnp.where(grp == 2, s8, s16)))
    wnd = jnp.where(grp == 0, 2, jnp.where(grp == 1, 4, jnp.where(grp == 2, 8, 16)))
    cnt = jnp.minimum(row + 1, wnd).astype(F32)
    pooled = (wsum / cnt - u).astype(BF16)
    mixed = jnp.dot(pooled, w_ref[...], preferred_element_type=F32)
    o_ref[0] = (mixed * scale_ref[...]).astype(BF16)


def _pool(u, w_blockdiag, scale):
    B = u.shape[0]
    return pl.pallas_call(
        _pool_kernel,
        grid=(B,),
        in_specs=[
            pl.BlockSpec((1, SEQ, GROUP_W), lambda b: (b, 0, 0)),
            _resident((GROUP_W, GROUP_W)),
            _resident((1, GROUP_W)),
        ],
        out_specs=pl.BlockSpec((1, SEQ, GROUP_W), lambda b: (b, 0, 0)),
        out_shape=jax.ShapeDtypeStruct((B, SEQ, GROUP_W), BF16),
        compiler_params=_cparams("parallel"),
        name="pool",
    )(u, w_blockdiag, scale)


def _conv_kernel(u_ref, w_ref, b_ref, lg_ref, lb_ref, o_ref, hs_ref):
    a = u_ref[0, :, 0:GROUP_W]
    g = u_ref[0, :, GROUP_W:]
    hs_ref[0, 0:CONV_PAD, :] = jnp.zeros((CONV_PAD, GROUP_W), F32)
    hs_ref[0, CONV_PAD:, :] = a * jax.nn.sigmoid(g)
    for b in range(1, SUBLANES):
        hs_ref[b, 0:CONV_COPY_ROWS, :] = hs_ref[0, b:b + CONV_COPY_ROWS, :]
    first = CONV_PAD - (CONV_KERNEL - 1)

    def chunk(c, carry):
        r0 = pl.multiple_of(c * CONV_ROWS, CONV_ROWS)
        acc = None
        for j in range(CONV_KERNEL):
            a_, b_ = divmod(first + j, SUBLANES)
            term = w_ref[j:j + 1, :] * hs_ref[b_, pl.ds(r0 + SUBLANES * a_, CONV_ROWS), :]
            acc = term if acc is None else acc + term
        y = acc + b_ref[...]
        mu = jnp.mean(y, axis=-1, keepdims=True)
        var = jnp.mean(jnp.square(y - mu), axis=-1, keepdims=True)
        z = (y - mu) * lax.rsqrt(var + LN_EPS) * lg_ref[...] + lb_ref[...]
        o_ref[0, pl.ds(r0, CONV_ROWS), :] = jax.nn.silu(z).astype(BF16)
        return carry

    lax.fori_loop(0, SEQ // CONV_ROWS, chunk, 0, unroll=2)


def _conv(u, w, b, ln_g, ln_b):
    B = u.shape[0]
    return pl.pallas_call(
        _conv_kernel,
        grid=(B,),
        in_specs=[
            pl.BlockSpec((1, SEQ, 2 * GROUP_W), lambda b_: (b_, 0, 0)),
            _resident((CONV_KERNEL, GROUP_W)),
            _resident((1, GROUP_W)), _resident((1, GROUP_W)), _resident((1, GROUP_W)),
        ],
        out_specs=pl.BlockSpec((1, SEQ, GROUP_W), lambda b_: (b_, 0, 0)),
        out_shape=jax.ShapeDtypeStruct((B, SEQ, GROUP_W), BF16),
        scratch_shapes=[pltpu.VMEM((SUBLANES, SEQ + CONV_PAD, GROUP_W), F32)],
        compiler_params=_cparams("parallel"),
        name="conv",
    )(u, w, b, ln_g, ln_b)


def _scores_t(k, q):
    return lax.dot_general(k, q, (((1,), (1,)), ((), ())), preferred_element_type=F32)


class _AttnUnit:
    def __init__(self, n_chunks, score_fn, vt_fn, s_ref, done_fn):
        self.n_chunks, self.score_fn, self.vt_fn, self.s_ref, self.done_fn = n_chunks, score_fn, vt_fn, s_ref, done_fn
        self.m = None
        self.acc = None

    def score(self, j):
        s = self.score_fn(j)
        self.s_ref[j * KEY_CHUNK:(j + 1) * KEY_CHUNK, :] = s
        cm = s.max(axis=0, keepdims=True)
        self.m = cm if self.m is None else jnp.maximum(self.m, cm)

    def apply(self, j):
        p = jnp.exp2(self.s_ref[j * KEY_CHUNK:(j + 1) * KEY_CHUNK, :] - self.m).astype(BF16)
        pv = jnp.dot(self.vt_fn(j), p, preferred_element_type=F32)
        self.acc = pv if self.acc is None else self.acc + pv

    def finish(self):
        self.done_fn(self.acc[0:HEAD_DIM] / self.acc[HEAD_DIM:HEAD_DIM + 1])


def _run_interleaved(units):
    prev = None
    for unit in list(units) + [None]:
        n_score = unit.n_chunks if unit is not None else 0
        n_apply = prev.n_chunks if prev is not None else 0
        for j in range(max(n_score, n_apply)):
            if j < n_score:
                unit.score(j)
            if j < n_apply:
                prev.apply(j)
        if prev is not None:
            prev.finish()
        prev = unit


def _store_values_t(v_slab_ref_slice, vt_ref, pair):
    vt = v_slab_ref_slice.T
    ones_blk = (lax.broadcasted_iota(jnp.int32, (BF16_ROWS, SEQ), 0) == 0).astype(BF16)
    for hh in range(2):
        h = 2 * pair + hh
        vt_ref[h, 0:HEAD_DIM, :] = vt[hh * HEAD_DIM:(hh + 1) * HEAD_DIM].astype(BF16)
        vt_ref[h, HEAD_DIM:VT_ROWS, :] = ones_blk


def _store_heads(o_ref, r0, outs_t):
    for pair in range(2):
        both = jnp.concatenate(outs_t[2 * pair:2 * pair + 2], axis=0)
        o_ref[0, r0:r0 + Q_BLOCK, pair * LANES:(pair + 1) * LANES] = both.T.astype(BF16)


def _moba_kernel(qkv_ref, causal_ref, o_ref, qa_ref, ka_ref, vt_ref, s_ref):
    S = SEQ
    blk_row = lax.broadcasted_iota(jnp.int32, (MOBA_NB, S), 0)
    q_blk = lax.broadcasted_iota(jnp.int32, (MOBA_NB, S), 1) // MOBA_BLOCK
    past = blk_row < q_blk
    key_blk = lax.broadcasted_iota(jnp.int32, (S, HEAD_DIM), 0) // MOBA_BLOCK
    onehot = (lax.broadcasted_iota(jnp.int32, (S, HEAD_DIM), 1) == key_blk).astype(F32)

    for pair in range(2):
        _store_values_t(qkv_ref[0, :, 2 * GROUP_W + pair * LANES: 2 * GROUP_W + (pair + 1) * LANES], vt_ref, pair)
    for h in range(N_HEADS):
        q = qkv_ref[0, :, h * HEAD_DIM:(h + 1) * HEAD_DIM]
        k = qkv_ref[0, :, GROUP_W + h * HEAD_DIM: GROUP_W + (h + 1) * HEAD_DIM]
        kmean = jnp.mean(k.reshape(MOBA_NB, MOBA_BLOCK, HEAD_DIM), axis=1)
        gate = lax.dot_general(kmean, q, (((1,), (1,)), ((), ())),
                               precision=lax.Precision.HIGHEST, preferred_element_type=F32)
        gate = jnp.where(past, gate, NEG_INF)
        rank = jnp.zeros((MOBA_NB, S), jnp.int32)
        for jp in range(MOBA_NB):
            gj = gate[jp:jp + 1, :]
            beats = (gj > gate) | ((gj == gate) & (jp < blk_row))
            rank = rank + beats.astype(jnp.int32)
        keep = (past & (rank < MOBA_TOPK)) | (blk_row == q_blk)
        bias_t = jnp.where(keep, 0.0, NEG_INF)
        bias_t = jnp.concatenate([bias_t, jnp.zeros((LANES - MOBA_NB, S), F32)], axis=0)
        bias = bias_t.T
        qa_ref[h] = jnp.concatenate([q * Q_SCALE, bias[:, 0:HEAD_DIM]], axis=1).astype(BF16)
        ka_ref[h] = jnp.concatenate([k, onehot], axis=1).astype(BF16)

    def make_unit(i, h, outs):
        r0 = i * Q_BLOCK

        def score(j):
            s = _scores_t(ka_ref[h, j * KEY_CHUNK:(j + 1) * KEY_CHUNK, :], qa_ref[h, r0:r0 + Q_BLOCK, :])
            d = j - i * Q_CHUNKS
            return s + causal_ref[d * KEY_CHUNK:(d + 1) * KEY_CHUNK, :] if d >= 0 else s

        def done(out_t):
            outs.append(out_t)
            if len(outs) == N_HEADS:
                _store_heads(o_ref, r0, outs)

        return _AttnUnit((i + 1) * Q_CHUNKS, score, lambda j: vt_ref[h, :, j * KEY_CHUNK:(j + 1) * KEY_CHUNK],
                         s_ref.at[(i * N_HEADS + h) % 2], done)

    units = []
    for i in range(SEQ // Q_BLOCK):
        outs = []
        units += [make_unit(i, h, outs) for h in range(N_HEADS)]
    _run_interleaved(units)


def _moba(qkv, causal_bias_t):
    B = qkv.shape[0]
    return pl.pallas_call(
        _moba_kernel,
        grid=(B,),
        in_specs=[
            pl.BlockSpec((1, SEQ, 3 * GROUP_W), lambda b: (b, 0, 0)),
            _resident((Q_BLOCK, Q_BLOCK)),
        ],
        out_specs=pl.BlockSpec((1, SEQ, GROUP_W), lambda b: (b, 0, 0)),
        out_shape=jax.ShapeDtypeStruct((B, SEQ, GROUP_W), BF16),
        scratch_shapes=[
            pltpu.VMEM((N_HEADS, SEQ, 2 * HEAD_DIM), BF16),
            pltpu.VMEM((N_HEADS, SEQ, 2 * HEAD_DIM), BF16),
            pltpu.VMEM((N_HEADS, VT_ROWS, SEQ), BF16),
            pltpu.VMEM((2, SEQ, Q_BLOCK), F32),
        ],
        compiler_params=_cparams("parallel"),
        name="moba",
    )(qkv, causal_bias_t)


def _dil_kernel(qkv_ref, bias_ref, o_ref, q_ref, k_ref, vt_ref, s_ref):
    for pair in range(2):
        _store_values_t(qkv_ref[0, :, 2 * GROUP_W + pair * LANES: 2 * GROUP_W + (pair + 1) * LANES], vt_ref, pair)
    for h in range(N_HEADS):
        q_ref[h] = (qkv_ref[0, :, h * HEAD_DIM:(h + 1) * HEAD_DIM] * Q_SCALE).astype(BF16)
        k_ref[h] = qkv_ref[0, :, GROUP_W + h * HEAD_DIM: GROUP_W + (h + 1) * HEAD_DIM].astype(BF16)

    def make_unit(i, h, outs):
        r0 = i * Q_BLOCK
        bias0 = SEQ - (r0 + Q_BLOCK)

        def score(j):
            keys = slice(j * KEY_CHUNK, (j + 1) * KEY_CHUNK)
            s = _scores_t(k_ref[h, keys, :], q_ref[h, r0:r0 + Q_BLOCK, :])
            return s + bias_ref[bias0 + j * KEY_CHUNK: bias0 + (j + 1) * KEY_CHUNK, :]

        def done(out_t):
            outs.append(out_t)
            if len(outs) == N_HEADS:
                _store_heads(o_ref, r0, outs)

        return _AttnUnit((i + 1) * Q_CHUNKS, score, lambda j: vt_ref[h, :, j * KEY_CHUNK:(j + 1) * KEY_CHUNK],
                         s_ref.at[(i * N_HEADS + h) % 2], done)

    units = []
    for i in range(SEQ // Q_BLOCK):
        outs = []
        units += [make_unit(i, h, outs) for h in range(N_HEADS)]
    _run_interleaved(units)


def _dil(qkv, dist_bias_t):
    B = qkv.shape[0]
    return pl.pallas_call(
        _dil_kernel,
        grid=(B,),
        in_specs=[
            pl.BlockSpec((1, SEQ, 3 * GROUP_W), lambda b: (b, 0, 0)),
            _resident((SEQ, Q_BLOCK)),
        ],
        out_specs=pl.BlockSpec((1, SEQ, GROUP_W), lambda b: (b, 0, 0)),
        out_shape=jax.ShapeDtypeStruct((B, SEQ, GROUP_W), BF16),
        scratch_shapes=[
            pltpu.VMEM((N_HEADS, SEQ, HEAD_DIM), BF16),
            pltpu.VMEM((N_HEADS, SEQ, HEAD_DIM), BF16),
            pltpu.VMEM((N_HEADS, VT_ROWS, SEQ), BF16),
            pltpu.VMEM((2, SEQ, Q_BLOCK), F32),
        ],
        compiler_params=_cparams("parallel"),
        name="dilated",
    )(qkv, dist_bias_t)


def _outproj_kernel(x_ref, yp_ref, ym_ref, yd_ref, yc_ref, w_ref, o_ref):
    mix = jnp.concatenate([yp_ref[...], ym_ref[...], yd_ref[...], yc_ref[...]], axis=1)
    o_ref[...] = x_ref[...] + jnp.dot(mix, w_ref[...], preferred_element_type=F32)


def _outproj(x2d, yp, ym, yd, yc, w_out, *, tm=512):
    n_tok = x2d.shape[0]
    grp = pl.BlockSpec((tm, GROUP_W), lambda i: (i, 0))
    return pl.pallas_call(
        _outproj_kernel,
        grid=(n_tok // tm,),
        in_specs=[pl.BlockSpec((tm, D_MODEL), lambda i: (i, 0)), grp, grp, grp, grp,
                  _resident((D_MODEL, D_MODEL))],
        out_specs=pl.BlockSpec((tm, D_MODEL), lambda i: (i, 0)),
        out_shape=jax.ShapeDtypeStruct((n_tok, D_MODEL), F32),
        compiler_params=_cparams("parallel"),
        name="outproj",
    )(x2d, yp, ym, yd, yc, w_out)


def _causal_bias_t():
    r = np.arange(Q_BLOCK)
    return jnp.asarray(np.where(r[:, None] <= r[None, :], 0.0, NEG_INF), F32)


def _dilated_distance_bias_t():
    r = np.arange(Q_BLOCK)[None, :]
    x = np.arange(SEQ)[:, None]
    d = r - (x - (SEQ - Q_BLOCK))
    cnt = ((d >= 0) & (d <= 128)).astype(np.int64)
    cnt = cnt + ((d >= 0) & (d % 4 == 0) & (d <= 512))
    cnt = cnt + ((d >= 0) & (d % 16 == 0) & (d <= 2048))
    return jnp.asarray(np.where(cnt > 0, np.log2(np.maximum(cnt, 1)), NEG_INF), F32)


def _pool_blockdiag(pool_w):
    n = len(POOL_WINDOWS)
    eye = jnp.eye(n, dtype=pool_w.dtype)
    return jnp.einsum('gcd,gh->gchd', pool_w, eye).reshape(n * POOL_GROUP, n * POOL_GROUP)


def kernel(x, positions, ffn1_norm, ffn1_gate, ffn1_up, ffn1_down, mix_norm, w_in, pool_w, pool_scale,
           conv_w, conv_b, conv_ln_g, conv_ln_b, w_out, ffn2_norm, ffn2_gate, ffn2_up, ffn2_down, final_norm):
    B, S, D = x.shape
    assert (S, D) == (SEQ, D_MODEL)
    rope_c, rope_su, rope_sd = _rope_tables(positions)
    causal_t = _causal_bias_t()
    dist_bias_t = _dilated_distance_bias_t()
    fin = final_norm.reshape(1, D)
    row = lambda t: t.reshape(1, -1)
    x2d = x.reshape(B * S, D)
    for l in range(DEPTH):
        x2d = _ffn(x2d, row(ffn1_norm[l]), ffn1_gate[l].astype(BF16), ffn1_up[l].astype(BF16),
                   ffn1_down[l].astype(BF16), fin, final_norm=False)
        u_pool, qkv_m, qkv_d, u_conv = _inproj(x2d.reshape(B, S, D), row(mix_norm[l]), w_in[l].astype(BF16),
                                               rope_c, rope_su, rope_sd)
        y_pool = _pool(u_pool, _pool_blockdiag(pool_w[l]).astype(BF16), row(pool_scale[l]))
        y_moba = _moba(qkv_m, causal_t)
        y_dil = _dil(qkv_d, dist_bias_t)
        y_conv = _conv(u_conv, conv_w[l], row(conv_b[l]), row(conv_ln_g[l]), row(conv_ln_b[l]))
        flat = lambda t: t.reshape(B * S, GROUP_W)
        x2d = _outproj(x2d, flat(y_pool), flat(y_moba), flat(y_dil), flat(y_conv), w_out[l].astype(BF16))
        x2d = _ffn(x2d, row(ffn2_norm[l]), ffn2_gate[l].astype(BF16), ffn2_up[l].astype(BF16),
                   ffn2_down[l].astype(BF16), fin, final_norm=(l == DEPTH - 1))
    return x2d.reshape(B, S, D)
```

```python
import functools
import math

import numpy as np
import jax
import jax.numpy as jnp
from jax import lax
from jax.experimental import pallas as pl
from jax.experimental.pallas import tpu as pltpu

F32 = jnp.float32
BF16 = jnp.bfloat16

D_MODEL = 1024
SEQ = 2048
DEPTH = 2
HEAD_DIM = 64
N_HEADS = 4
GROUP_W = 256
POOL_WINDOWS = (2, 4, 8, 16)
POOL_GROUP = 64
MOBA_BLOCK = 256
MOBA_NB = SEQ // MOBA_BLOCK
MOBA_TOPK = 3
CONV_KERNEL = 31
ROPE_THETA = 500000.0
ROPE_DIMS = 16
D_FF = 2816
FF_CHUNK = 1408
FFN_ROWS = 512
D_IN = 2304
RMS_EPS = 1e-6
LN_EPS = 1e-5
NEG_INF = -1e30
ATTN_SCALE = HEAD_DIM ** -0.5
LOG2E = math.log2(math.e)
Q_SCALE = ATTN_SCALE * LOG2E
Q_BLOCK = 256
KEY_CHUNK = 256
Q_CHUNKS = Q_BLOCK // KEY_CHUNK
ATTN_STREAMS = 4
LANES = 128
SUBLANES = 8
BF16_ROWS = 16
VT_ROWS = HEAD_DIM + BF16_ROWS
CONV_PAD = 32
CONV_COPY_ROWS = SEQ + CONV_PAD - SUBLANES
CONV_ROWS = 128

VMEM_LIMIT = 56 * 1024 * 1024


def _cparams(*sem):
    return pltpu.CompilerParams(dimension_semantics=sem, vmem_limit_bytes=VMEM_LIMIT)


def _rms(x, g):
    return x * lax.rsqrt(jnp.mean(x * x, axis=-1, keepdims=True) + RMS_EPS) * g


def _resident(shape):
    return pl.BlockSpec(shape, lambda *_: (0,) * len(shape), pipeline_mode=pl.Buffered(1))


def _half_step_ffn(x, g_ref, wg_ref, wu_ref, wd_ref, hid_ref):
    xn = _rms(x, g_ref[...]).astype(BF16)
    for c in range(D_FF // FF_CHUNK):
        cols = slice(c * FF_CHUNK, (c + 1) * FF_CHUNK)
        gate = jnp.dot(xn, wg_ref[:, cols], preferred_element_type=F32)
        up = jnp.dot(xn, wu_ref[:, cols], preferred_element_type=F32)
        hid_ref[:, cols] = (jax.nn.silu(gate) * up).astype(BF16)
    return x + 0.5 * jnp.dot(hid_ref[...], wd_ref[...], preferred_element_type=F32)


def _ffn_kernel(x_ref, g_ref, wg_ref, wu_ref, wd_ref, o_ref, hid_ref):
    o_ref[...] = _half_step_ffn(x_ref[...], g_ref, wg_ref, wu_ref, wd_ref, hid_ref)


def _mix_ffn_kernel(x_ref, yp_ref, ym_ref, yd_ref, yc_ref, wo_ref, g_ref, wg_ref, wu_ref, wd_ref, fin_ref,
                    o_ref, hid_ref, *, final_norm):
    mix = jnp.concatenate([yp_ref[...], ym_ref[...], yd_ref[...], yc_ref[...]], axis=1)
    x = x_ref[...] + jnp.dot(mix, wo_ref[...], preferred_element_type=F32)
    y = _half_step_ffn(x, g_ref, wg_ref, wu_ref, wd_ref, hid_ref)
    if final_norm:
        y = _rms(y, fin_ref[...])
    o_ref[...] = y


def _ffn_weight_specs():
    return [_resident((1, D_MODEL)), _resident((D_MODEL, D_FF)), _resident((D_MODEL, D_FF)),
            _resident((D_FF, D_MODEL))]


def _ffn(x2d, norm_g, wg, wu, wd, *, tm=FFN_ROWS):
    n_tok = x2d.shape[0]
    tile = pl.BlockSpec((tm, D_MODEL), lambda i: (i, 0))
    return pl.pallas_call(
        _ffn_kernel,
        grid=(n_tok // tm,),
        in_specs=[tile] + _ffn_weight_specs(),
        out_specs=tile,
        out_shape=jax.ShapeDtypeStruct((n_tok, D_MODEL), F32),
        scratch_shapes=[pltpu.VMEM((tm, D_FF), BF16)],
        compiler_params=_cparams("parallel"),
        name="ffn",
    )(x2d, norm_g, wg, wu, wd)


def _mix_ffn(x2d, yp, ym, yd, yc, w_out, norm_g, wg, wu, wd, fin_g, *, final_norm, tm=FFN_ROWS):
    n_tok = x2d.shape[0]
    tile = pl.BlockSpec((tm, D_MODEL), lambda i: (i, 0))
    grp = pl.BlockSpec((tm, GROUP_W), lambda i: (i, 0))
    return pl.pallas_call(
        functools.partial(_mix_ffn_kernel, final_norm=final_norm),
        grid=(n_tok // tm,),
        in_specs=[tile, grp, grp, grp, grp, _resident((D_MODEL, D_MODEL))] + _ffn_weight_specs()
                 + [_resident((1, D_MODEL))],
        out_specs=tile,
        out_shape=jax.ShapeDtypeStruct((n_tok, D_MODEL), F32),
        scratch_shapes=[pltpu.VMEM((tm, D_FF), BF16)],
        compiler_params=_cparams("parallel"),
        name="mix_ffn",
    )(x2d, yp, ym, yd, yc, w_out, norm_g, wg, wu, wd, fin_g)


def _rope_table_kernel(pos_ref, inv_ref, c_ref, su_ref, sd_ref):
    half = ROPE_DIMS // 2
    ang = pos_ref[0].astype(F32) * inv_ref[...]
    cos, sin = jnp.cos(ang), jnp.sin(ang)
    zeros = lambda n: jnp.zeros((n, SEQ), F32)
    rest = HEAD_DIM - ROPE_DIMS
    c = jnp.concatenate([cos, cos, jnp.ones((rest, SEQ), F32)] * 2, axis=0)
    s_up = jnp.concatenate([zeros(half), sin, zeros(rest)] * 2, axis=0)
    s_dn = jnp.concatenate([-sin, zeros(half), zeros(rest)] * 2, axis=0)
    c_ref[0] = c.T
    su_ref[0] = s_up.T
    sd_ref[0] = s_dn.T


def _rope_tables(positions):
    B = positions.shape[0]
    half = ROPE_DIMS // 2
    inv = ROPE_THETA ** (-jnp.arange(0, ROPE_DIMS, 2, dtype=F32) / ROPE_DIMS)
    table = jax.ShapeDtypeStruct((B, SEQ, LANES), F32)
    out_spec = pl.BlockSpec((1, SEQ, LANES), lambda b: (b, 0, 0))
    return pl.pallas_call(
        _rope_table_kernel,
        grid=(B,),
        in_specs=[pl.BlockSpec((1, 1, SEQ), lambda b: (b, 0, 0)), _resident((half, 1))],
        out_specs=[out_spec] * 3,
        out_shape=[table] * 3,
        compiler_params=_cparams("parallel"),
        name="rope_tables",
    )(positions.reshape(B, 1, SEQ), inv.reshape(half, 1))


def _rope(t, c, s_up, s_dn):
    half = ROPE_DIMS // 2
    return t * c + pltpu.roll(t, half, axis=1) * s_up + pltpu.roll(t, LANES - half, axis=1) * s_dn


def _inproj_kernel(x_ref, g_ref, w_ref, c_ref, su_ref, sd_ref, pool_ref, moba_ref, dil_ref, conv_ref):
    xn = _rms(x_ref[0], g_ref[...]).astype(BF16)
    h = jnp.dot(xn, w_ref[...], preferred_element_type=F32)
    c, su, sd = c_ref[0], su_ref[0], sd_ref[0]
    pool_ref[0] = h[:, 0:GROUP_W]
    for out_ref, off in ((moba_ref, GROUP_W), (dil_ref, 4 * GROUP_W)):
        for part in range(6):
            piece = h[:, off + part * LANES: off + (part + 1) * LANES]
            if part < 4:
                piece = _rope(piece, c, su, sd)
            out_ref[0, :, part * LANES:(part + 1) * LANES] = piece
    conv_ref[0] = h[:, 7 * GROUP_W:]


def _inproj(x, norm_g, w_in, rope_c, rope_su, rope_sd, *, tm=512):
    B = x.shape[0]
    tok = lambda b, i: (b, i, 0)
    return pl.pallas_call(
        _inproj_kernel,
        grid=(B, SEQ // tm),
        in_specs=[
            pl.BlockSpec((1, tm, D_MODEL), tok),
            _resident((1, D_MODEL)),
            _resident((D_MODEL, D_IN)),
            pl.BlockSpec((1, tm, LANES), tok),
            pl.BlockSpec((1, tm, LANES), tok),
            pl.BlockSpec((1, tm, LANES), tok),
        ],
        out_specs=[
            pl.BlockSpec((1, tm, GROUP_W), tok),
            pl.BlockSpec((1, tm, 3 * GROUP_W), tok),
            pl.BlockSpec((1, tm, 3 * GROUP_W), tok),
            pl.BlockSpec((1, tm, 2 * GROUP_W), tok),
        ],
        out_shape=[
            jax.ShapeDtypeStruct((B, SEQ, GROUP_W), F32),
            jax.ShapeDtypeStruct((B, SEQ, 3 * GROUP_W), F32),
            jax.ShapeDtypeStruct((B, SEQ, 3 * GROUP_W), F32),
            jax.ShapeDtypeStruct((B, SEQ, 2 * GROUP_W), F32),
        ],
        compiler_params=_cparams("parallel", "parallel"),
        name="inproj",
    )(x, norm_g, w_in, rope_c, rope_su, rope_sd)


def _pool_kernel(u_ref, w_ref, scale_ref, o_ref):
    u = u_ref[0]
    row = lax.broadcasted_iota(jnp.int32, u.shape, 0)
    lane = lax.broadcasted_iota(jnp.int32, u.shape, 1)

    def shifted(x, s):
        return jnp.where(row >= s, pltpu.roll(x, s, axis=0), 0.0)

    s2 = u + shifted(u, 1)
    s4 = s2 + shifted(s2, 2)
    s8 = s4 + shifted(s4, 4)
    s16 = s8 + shifted(s8, 8)
    grp = lane // POOL_GROUP
    wsum = jnp.where(grp == 0, s2, jnp.where(grp == 1, s4, jnp.where(grp == 2, s8, s16)))
    wnd = jnp.where(grp == 0, 2, jnp.where(grp == 1, 4, jnp.where(grp == 2, 8, 16)))
    cnt = jnp.minimum(row + 1, wnd).astype(F32)
    pooled = (wsum / cnt - u).astype(BF16)
    mixed = jnp.dot(pooled, w_ref[...], preferred_element_type=F32)
    o_ref[0] = (mixed * scale_ref[...]).astype(BF16)


def _pool(u, w_blockdiag, scale):
    B = u.shape[0]
    return pl.pallas_call(
        _pool_kernel,
        grid=(B,),
        in_specs=[
            pl.BlockSpec((1, SEQ, GROUP_W), lambda b: (b, 0, 0)),
            _resident((GROUP_W, GROUP_W)),
            _resident((1, GROUP_W)),
        ],
        out_specs=pl.BlockSpec((1, SEQ, GROUP_W), lambda b: (b, 0, 0)),
        out_shape=jax.ShapeDtypeStruct((B, SEQ, GROUP_W), BF16),
        compiler_params=_cparams("parallel"),
        name="pool",
    )(u, w_blockdiag, scale)


def _conv_kernel(u_ref, w_ref, b_ref, lg_ref, lb_ref, o_ref, hs_ref):
    a = u_ref[0, :, 0:GROUP_W]
    g = u_ref[0, :, GROUP_W:]
    hs_ref[0, 0:CONV_PAD, :] = jnp.zeros((CONV_PAD, GROUP_W), F32)
    hs_ref[0, CONV_PAD:, :] = a * jax.nn.sigmoid(g)
    for b in range(1, SUBLANES):
        hs_ref[b, 0:CONV_COPY_ROWS, :] = hs_ref[0, b:b + CONV_COPY_ROWS, :]
    first = CONV_PAD - (CONV_KERNEL - 1)

    def chunk(c, carry):
        r0 = pl.multiple_of(c * CONV_ROWS, CONV_ROWS)
        acc = None
        for j in range(CONV_KERNEL):
            a_, b_ = divmod(first + j, SUBLANES)
            term = w_ref[j:j + 1, :] * hs_ref[b_, pl.ds(r0 + SUBLANES * a_, CONV_ROWS), :]
            acc = term if acc is None else acc + term
        y = acc + b_ref[...]
        mu = jnp.mean(y, axis=-1, keepdims=True)
        var = jnp.mean(jnp.square(y - mu), axis=-1, keepdims=True)
        z = (y - mu) * lax.rsqrt(var + LN_EPS) * lg_ref[...] + lb_ref[...]
        o_ref[0, pl.ds(r0, CONV_ROWS), :] = jax.nn.silu(z).astype(BF16)
        return carry

    lax.fori_loop(0, SEQ // CONV_ROWS, chunk, 0, unroll=2)


def _conv(u, w, b, ln_g, ln_b):
    B = u.shape[0]
    return pl.pallas_call(
        _conv_kernel,
        grid=(B,),
        in_specs=[
            pl.BlockSpec((1, SEQ, 2 * GROUP_W), lambda b_: (b_, 0, 0)),
            _resident((CONV_KERNEL, GROUP_W)),
            _resident((1, GROUP_W)), _resident((1, GROUP_W)), _resident((1, GROUP_W)),
        ],
        out_specs=pl.BlockSpec((1, SEQ, GROUP_W), lambda b_: (b_, 0, 0)),
        out_shape=jax.ShapeDtypeStruct((B, SEQ, GROUP_W), BF16),
        scratch_shapes=[pltpu.VMEM((SUBLANES, SEQ + CONV_PAD, GROUP_W), F32)],
        compiler_params=_cparams("parallel"),
        name="conv",
    )(u, w, b, ln_g, ln_b)


def _scores_t(k, q):
    return lax.dot_general(k, q, (((1,), (1,)), ((), ())), preferred_element_type=F32)


class _AttnUnit:
    def __init__(self, n_chunks, score_fn, vt_fn, s_ref, done_fn):
        self.n_chunks, self.score_fn, self.vt_fn, self.s_ref, self.done_fn = n_chunks, score_fn, vt_fn, s_ref, done_fn
        self.m = None
        self.acc = None

    def score(self, j):
        s = self.score_fn(j)
        self.s_ref[j * KEY_CHUNK:(j + 1) * KEY_CHUNK, :] = s
        cm = s.max(axis=0, keepdims=True)
        self.m = cm if self.m is None else jnp.maximum(self.m, cm)

    def apply(self, j):
        p = jnp.exp2(self.s_ref[j * KEY_CHUNK:(j + 1) * KEY_CHUNK, :] - self.m).astype(BF16)
        pv = jnp.dot(self.vt_fn(j), p, preferred_element_type=F32)
        self.acc = pv if self.acc is None else self.acc + pv

    def finish(self):
        self.done_fn(self.acc[0:HEAD_DIM] / self.acc[HEAD_DIM:HEAD_DIM + 1])


def _score_slot(unit_index):
    stream, pos = unit_index % ATTN_STREAMS, unit_index // ATTN_STREAMS
    return 2 * stream + pos % 2


def _pipeline(units):
    prev = None
    for unit in list(units) + [None]:
        n_score = unit.n_chunks if unit is not None else 0
        n_apply = prev.n_chunks if prev is not None else 0
        for j in range(max(n_score, n_apply)):
            if j < n_score:
                unit.score(j)
            if j < n_apply:
                prev.apply(j)
            yield
        if prev is not None:
            prev.finish()
        prev = unit


def _run_interleaved(units, streams=1):
    live = [_pipeline(units[s::streams]) for s in range(streams)]
    while live:
        live = [g for g in live if next(g, StopIteration) is not StopIteration]


def _store_values_t(v_slab_ref_slice, vt_ref, pair):
    vt = v_slab_ref_slice.T
    ones_blk = (lax.broadcasted_iota(jnp.int32, (BF16_ROWS, SEQ), 0) == 0).astype(BF16)
    for hh in range(2):
        h = 2 * pair + hh
        vt_ref[h, 0:HEAD_DIM, :] = vt[hh * HEAD_DIM:(hh + 1) * HEAD_DIM].astype(BF16)
        vt_ref[h, HEAD_DIM:VT_ROWS, :] = ones_blk


def _store_heads(o_ref, r0, outs_t):
    for pair in range(2):
        both = jnp.concatenate(outs_t[2 * pair:2 * pair + 2], axis=0)
        o_ref[0, r0:r0 + Q_BLOCK, pair * LANES:(pair + 1) * LANES] = both.T.astype(BF16)


def _moba_kernel(qkv_ref, causal_ref, o_ref, qa_ref, ka_ref, vt_ref, s_ref):
    S = SEQ
    blk_row = lax.broadcasted_iota(jnp.int32, (MOBA_NB, S), 0)
    q_blk = lax.broadcasted_iota(jnp.int32, (MOBA_NB, S), 1) // MOBA_BLOCK
    past = blk_row < q_blk
    key_blk = lax.broadcasted_iota(jnp.int32, (S, LANES), 0) // MOBA_BLOCK
    onehot = (lax.broadcasted_iota(jnp.int32, (S, LANES), 1) == key_blk).astype(BF16)
    lane_head = lax.broadcasted_iota(jnp.int32, (1, LANES), 1) // HEAD_DIM

    def split(t):
        hi = t.astype(BF16)
        return hi, (t - hi.astype(F32)).astype(BF16)

    for pair in range(2):
        _store_values_t(qkv_ref[0, :, 2 * GROUP_W + pair * LANES: 2 * GROUP_W + (pair + 1) * LANES], vt_ref, pair)
        q_slab = qkv_ref[0, :, pair * LANES:(pair + 1) * LANES] * Q_SCALE
        k_slab = qkv_ref[0, :, GROUP_W + pair * LANES: GROUP_W + (pair + 1) * LANES]
        kmean = jnp.mean(k_slab.reshape(MOBA_NB, MOBA_BLOCK, LANES), axis=1)
        kmean2 = jnp.concatenate([jnp.where(lane_head == hh, kmean, 0.0) for hh in range(2)], axis=0)
        q_hi, q_lo = split(q_slab)
        k_hi, k_lo = split(kmean2)
        gate2 = _scores_t(jnp.concatenate([k_hi, k_lo, k_hi], axis=1),
                          jnp.concatenate([q_hi, q_hi, q_lo], axis=1))
        for hh in range(2):
            h = 2 * pair + hh
            gate = jnp.where(past, gate2[hh * MOBA_NB:(hh + 1) * MOBA_NB], NEG_INF)
            rank = jnp.zeros((MOBA_NB, S), jnp.int32)
            for jp in range(MOBA_NB):
                gj = gate[jp:jp + 1, :]
                beats = (gj > gate) | ((gj == gate) & (jp < blk_row))
                rank = rank + beats.astype(jnp.int32)
            keep = (past & (rank < MOBA_TOPK)) | (blk_row == q_blk)
            bias_t = jnp.where(keep, 0.0, NEG_INF)
            bias_t = jnp.concatenate([bias_t, jnp.zeros((LANES - MOBA_NB, S), F32)], axis=0)
            qa_ref[h] = jnp.concatenate([q_hi, bias_t.T.astype(BF16)], axis=1)
            ka_ref[h] = jnp.concatenate([jnp.where(lane_head == hh, k_slab, 0.0).astype(BF16), onehot], axis=1)

    def make_unit(i, h, outs):
        r0 = i * Q_BLOCK

        def score(j):
            s = _scores_t(ka_ref[h, j * KEY_CHUNK:(j + 1) * KEY_CHUNK, :], qa_ref[h, r0:r0 + Q_BLOCK, :])
            d = j - i * Q_CHUNKS
            return s + causal_ref[d * KEY_CHUNK:(d + 1) * KEY_CHUNK, :] if d >= 0 else s

        def done(out_t):
            outs.append(out_t)
            if len(outs) == N_HEADS:
                _store_heads(o_ref, r0, outs)

        return _AttnUnit((i + 1) * Q_CHUNKS, score, lambda j: vt_ref[h, :, j * KEY_CHUNK:(j + 1) * KEY_CHUNK],
                         s_ref.at[_score_slot(i * N_HEADS + h)], done)

    units = []
    for i in range(SEQ // Q_BLOCK):
        outs = []
        units += [make_unit(i, h, outs) for h in range(N_HEADS)]
    _run_interleaved(units, ATTN_STREAMS)


def _moba(qkv, causal_bias_t):
    B = qkv.shape[0]
    return pl.pallas_call(
        _moba_kernel,
        grid=(B,),
        in_specs=[
            pl.BlockSpec((1, SEQ, 3 * GROUP_W), lambda b: (b, 0, 0)),
            _resident((Q_BLOCK, Q_BLOCK)),
        ],
        out_specs=pl.BlockSpec((1, SEQ, GROUP_W), lambda b: (b, 0, 0)),
        out_shape=jax.ShapeDtypeStruct((B, SEQ, GROUP_W), BF16),
        scratch_shapes=[
            pltpu.VMEM((N_HEADS, SEQ, 2 * LANES), BF16),
            pltpu.VMEM((N_HEADS, SEQ, 2 * LANES), BF16),
            pltpu.VMEM((N_HEADS, VT_ROWS, SEQ), BF16),
            pltpu.VMEM((2 * ATTN_STREAMS, SEQ, Q_BLOCK), F32),
        ],
        compiler_params=_cparams("parallel"),
        name="moba",
    )(qkv, causal_bias_t)


def _dil_kernel(qkv_ref, bias_ref, o_ref, q_ref, k_ref, vt_ref, s_ref):
    for pair in range(2):
        _store_values_t(qkv_ref[0, :, 2 * GROUP_W + pair * LANES: 2 * GROUP_W + (pair + 1) * LANES], vt_ref, pair)
    lane_head = lax.broadcasted_iota(jnp.int32, (1, LANES), 1) // HEAD_DIM
    for pair in range(2):
        q_ref[pair] = (qkv_ref[0, :, pair * LANES:(pair + 1) * LANES] * Q_SCALE).astype(BF16)
        k_slab = qkv_ref[0, :, GROUP_W + pair * LANES: GROUP_W + (pair + 1) * LANES]
        for hh in range(2):
            k_ref[2 * pair + hh] = jnp.where(lane_head == hh, k_slab, 0.0).astype(BF16)

    def make_unit(i, h, outs):
        r0 = i * Q_BLOCK
        bias0 = SEQ - (r0 + Q_BLOCK)

        def score(j):
            keys = slice(j * KEY_CHUNK, (j + 1) * KEY_CHUNK)
            s = _scores_t(k_ref[h, keys, :], q_ref[h // 2, r0:r0 + Q_BLOCK, :])
            return s + bias_ref[bias0 + j * KEY_CHUNK: bias0 + (j + 1) * KEY_CHUNK, :]

        def done(out_t):
            outs.append(out_t)
            if len(outs) == N_HEADS:
                _store_heads(o_ref, r0, outs)

        return _AttnUnit((i + 1) * Q_CHUNKS, score, lambda j: vt_ref[h, :, j * KEY_CHUNK:(j + 1) * KEY_CHUNK],
                          s_ref.at[_score_slot(i * N_HEADS + h)], done)

    units = []
    for i in range(SEQ // Q_BLOCK):
        outs = []
        units += [make_unit(i, h, outs) for h in range(N_HEADS)]
    _run_interleaved(units, ATTN_STREAMS)


def _dil(qkv, dist_bias_t):
    B = qkv.shape[0]
    return pl.pallas_call(
        _dil_kernel,
        grid=(B,),
        in_specs=[
            pl.BlockSpec((1, SEQ, 3 * GROUP_W), lambda b: (b, 0, 0)),
            _resident((SEQ, Q_BLOCK)),
        ],
        out_specs=pl.BlockSpec((1, SEQ, GROUP_W), lambda b: (b, 0, 0)),
        out_shape=jax.ShapeDtypeStruct((B, SEQ, GROUP_W), BF16),
        scratch_shapes=[
            pltpu.VMEM((N_HEADS // 2, SEQ, LANES), BF16),
            pltpu.VMEM((N_HEADS, SEQ, LANES), BF16),
            pltpu.VMEM((N_HEADS, VT_ROWS, SEQ), BF16),
            pltpu.VMEM((2 * ATTN_STREAMS, SEQ, Q_BLOCK), F32),
        ],
        compiler_params=_cparams("parallel"),
        name="dilated",
    )(qkv, dist_bias_t)


def _causal_bias_t():
    r = np.arange(Q_BLOCK)
    return jnp.asarray(np.where(r[:, None] <= r[None, :], 0.0, NEG_INF), F32)


def _dilated_distance_bias_t():
    r = np.arange(Q_BLOCK)[None, :]
    x = np.arange(SEQ)[:, None]
    d = r - (x - (SEQ - Q_BLOCK))
    cnt = ((d >= 0) & (d <= 128)).astype(np.int64)
    cnt = cnt + ((d >= 0) & (d % 4 == 0) & (d <= 512))
    cnt = cnt + ((d >= 0) & (d % 16 == 0) & (d <= 2048))
    return jnp.asarray(np.where(cnt > 0, np.log2(np.maximum(cnt, 1)), NEG_INF), F32)


def _pool_blockdiag(pool_w):
    n = len(POOL_WINDOWS)
    eye = jnp.eye(n, dtype=pool_w.dtype)
    return jnp.einsum('gcd,gh->gchd', pool_w, eye).reshape(n * POOL_GROUP, n * POOL_GROUP)


def kernel(x, positions, ffn1_norm, ffn1_gate, ffn1_up, ffn1_down, mix_norm, w_in, pool_w, pool_scale,
           conv_w, conv_b, conv_ln_g, conv_ln_b, w_out, ffn2_norm, ffn2_gate, ffn2_up, ffn2_down, final_norm):
    B, S, D = x.shape
    assert (S, D) == (SEQ, D_MODEL)
    rope_c, rope_su, rope_sd = _rope_tables(positions)
    causal_t = _causal_bias_t()
    dist_bias_t = _dilated_distance_bias_t()
    fin = final_norm.reshape(1, D)
    row = lambda t: t.reshape(1, -1)
    x2d = x.reshape(B * S, D)
    for l in range(DEPTH):
        x2d = _ffn(x2d, row(ffn1_norm[l]), ffn1_gate[l].astype(BF16), ffn1_up[l].astype(BF16),
                   ffn1_down[l].astype(BF16))
        u_pool, qkv_m, qkv_d, u_conv = _inproj(x2d.reshape(B, S, D), row(mix_norm[l]), w_in[l].astype(BF16),
                                               rope_c, rope_su, rope_sd)
        y_pool = _pool(u_pool, _pool_blockdiag(pool_w[l]).astype(BF16), row(pool_scale[l]))
        y_moba = _moba(qkv_m, causal_t)
        y_dil = _dil(qkv_d, dist_bias_t)
        y_conv = _conv(u_conv, conv_w[l], row(conv_b[l]), row(conv_ln_g[l]), row(conv_ln_b[l]))
        flat = lambda t: t.reshape(B * S, GROUP_W)
        x2d = _mix_ffn(x2d, flat(y_pool), flat(y_moba), flat(y_dil), flat(y_conv), w_out[l].astype(BF16),
                       row(ffn2_norm[l]), ffn2_gate[l].astype(BF16), ffn2_up[l].astype(BF16),
                       ffn2_down[l].astype(BF16), fin, final_norm=(l == DEPTH - 1))
    return x2d.reshape(B, S, D)
```

```python
import functools
import math

import numpy as np
import jax
import jax.numpy as jnp
from jax import lax
from jax.experimental import pallas as pl
from jax.experimental.pallas import tpu as pltpu

F32 = jnp.float32
BF16 = jnp.bfloat16

D_MODEL = 1024
SEQ = 2048
DEPTH = 2
HEAD_DIM = 64
N_HEADS = 4
GROUP_W = 256
POOL_WINDOWS = (2, 4, 8, 16)
POOL_GROUP = 64
MOBA_BLOCK = 256
MOBA_NB = SEQ // MOBA_BLOCK
MOBA_TOPK = 3
CONV_KERNEL = 31
ROPE_THETA = 500000.0
ROPE_DIMS = 16
D_FF = 2816
FF_CHUNK = 256
FFN_ROWS = 512
D_IN = 2304
RMS_EPS = 1e-6
LN_EPS = 1e-5
NEG_INF = -1e30
ATTN_SCALE = HEAD_DIM ** -0.5
LOG2E = math.log2(math.e)
Q_SCALE = ATTN_SCALE * LOG2E
Q_BLOCK = 256
KEY_CHUNK = 256
Q_CHUNKS = Q_BLOCK // KEY_CHUNK
ATTN_STREAMS = 4
LANES = 128
SUBLANES = 8
BF16_ROWS = 16
VT_ROWS = HEAD_DIM + BF16_ROWS
CONV_PAD = 32
CONV_COPY_ROWS = SEQ + CONV_PAD - SUBLANES
CONV_ROWS = 128

VMEM_LIMIT = 56 * 1024 * 1024


def _cparams(*sem):
    return pltpu.CompilerParams(dimension_semantics=sem, vmem_limit_bytes=VMEM_LIMIT)


def _rms(x, g):
    return x * lax.rsqrt(jnp.mean(x * x, axis=-1, keepdims=True) + RMS_EPS) * g


def _resident(shape):
    return pl.BlockSpec(shape, lambda *_: (0,) * len(shape), pipeline_mode=pl.Buffered(1))


def _layer(shape, l):
    return pl.BlockSpec((None,) + tuple(shape), lambda *_: (l,) + (0,) * len(shape), pipeline_mode=pl.Buffered(1))


def _half_step_ffn(x, g_ref, wg_ref, wu_ref, wd_ref, hid_ref):
    xn = _rms(x, g_ref[...]).astype(BF16)
    for c in range(D_FF // FF_CHUNK):
        cols = slice(c * FF_CHUNK, (c + 1) * FF_CHUNK)
        gate = jnp.dot(xn, wg_ref[:, cols], preferred_element_type=F32)
        up = jnp.dot(xn, wu_ref[:, cols], preferred_element_type=F32)
        hid_ref[:, cols] = (jax.nn.silu(gate) * up).astype(BF16)
    return x + 0.5 * jnp.dot(hid_ref[...], wd_ref[...], preferred_element_type=F32)


def _rope(t, c, s_up, s_dn):
    half = ROPE_DIMS // 2
    return t * c + pltpu.roll(t, half, axis=1) * s_up + pltpu.roll(t, LANES - half, axis=1) * s_dn


def _ffn_inproj_kernel(x_ref, g_ref, wg_ref, wu_ref, wd_ref, gm_ref, wi_ref, c_ref, su_ref, sd_ref,
                       o_ref, pool_ref, moba_ref, dil_ref, conv_ref, hid_ref):
    x = _half_step_ffn(x_ref[...], g_ref, wg_ref, wu_ref, wd_ref, hid_ref)
    o_ref[...] = x
    h = jnp.dot(_rms(x, gm_ref[...]).astype(BF16), wi_ref[...], preferred_element_type=F32)
    c, su, sd = c_ref[...], su_ref[...], sd_ref[...]
    pool_ref[...] = h[:, 0:GROUP_W]
    for out_ref, off in ((moba_ref, GROUP_W), (dil_ref, 4 * GROUP_W)):
        for part in range(6):
            piece = h[:, off + part * LANES: off + (part + 1) * LANES]
            if part < 4:
                piece = _rope(piece, c, su, sd)
            out_ref[:, part * LANES:(part + 1) * LANES] = piece
    conv_ref[...] = h[:, 7 * GROUP_W:]


def _mix_ffn_kernel(x_ref, yp_ref, ym_ref, yd_ref, yc_ref, wo_ref, g_ref, wg_ref, wu_ref, wd_ref, fin_ref,
                    o_ref, hid_ref, *, final_norm):
    mix = jnp.concatenate([yp_ref[...], ym_ref[...], yd_ref[...], yc_ref[...]], axis=1)
    x = x_ref[...] + jnp.dot(mix, wo_ref[...], preferred_element_type=F32)
    y = _half_step_ffn(x, g_ref, wg_ref, wu_ref, wd_ref, hid_ref)
    if final_norm:
        y = _rms(y, fin_ref[...])
    o_ref[...] = y


def _ffn_weight_specs(l):
    return [_layer((1, D_MODEL), l), _layer((D_MODEL, D_FF), l), _layer((D_MODEL, D_FF), l),
            _layer((D_FF, D_MODEL), l)]


def _rows(width, tm):
    return pl.BlockSpec((tm, width), lambda i: (i, 0))


def _ffn_inproj(l, x2d, norm_g, wg, wu, wd, mix_g, w_in, rope_c, rope_su, rope_sd, *, tm=FFN_ROWS):
    n_tok = x2d.shape[0]
    widths = (D_MODEL, GROUP_W, 3 * GROUP_W, 3 * GROUP_W, 2 * GROUP_W)
    return pl.pallas_call(
        _ffn_inproj_kernel,
        grid=(n_tok // tm,),
        in_specs=[_rows(D_MODEL, tm)] + _ffn_weight_specs(l)
                 + [_layer((1, D_MODEL), l), _layer((D_MODEL, D_IN), l)] + [_rows(LANES, tm)] * 3,
        out_specs=[_rows(w, tm) for w in widths],
        out_shape=[jax.ShapeDtypeStruct((n_tok, w), F32) for w in widths],
        scratch_shapes=[pltpu.VMEM((tm, D_FF), BF16)],
        compiler_params=_cparams("parallel"),
        name="ffn_inproj",
    )(x2d, norm_g, wg, wu, wd, mix_g, w_in, rope_c, rope_su, rope_sd)


def _mix_ffn(l, x2d, yp, ym, yd, yc, w_out, norm_g, wg, wu, wd, fin_g, *, final_norm, tm=FFN_ROWS):
    n_tok = x2d.shape[0]
    return pl.pallas_call(
        functools.partial(_mix_ffn_kernel, final_norm=final_norm),
        grid=(n_tok // tm,),
        in_specs=[_rows(D_MODEL, tm)] + [_rows(GROUP_W, tm)] * 4 + [_layer((D_MODEL, D_MODEL), l)]
                 + _ffn_weight_specs(l) + [_resident((1, D_MODEL))],
        out_specs=_rows(D_MODEL, tm),
        out_shape=jax.ShapeDtypeStruct((n_tok, D_MODEL), F32),
        scratch_shapes=[pltpu.VMEM((tm, D_FF), BF16)],
        compiler_params=_cparams("parallel"),
        name="mix_ffn",
    )(x2d, yp, ym, yd, yc, w_out, norm_g, wg, wu, wd, fin_g)


def _rope_table_kernel(pos_ref, inv_ref, c_ref, su_ref, sd_ref):
    half = ROPE_DIMS // 2
    ang = pos_ref[0].astype(F32) * inv_ref[...]
    cos, sin = jnp.cos(ang), jnp.sin(ang)
    zeros = lambda n: jnp.zeros((n, SEQ), F32)
    rest = HEAD_DIM - ROPE_DIMS
    c = jnp.concatenate([cos, cos, jnp.ones((rest, SEQ), F32)] * 2, axis=0)
    s_up = jnp.concatenate([zeros(half), sin, zeros(rest)] * 2, axis=0)
    s_dn = jnp.concatenate([-sin, zeros(half), zeros(rest)] * 2, axis=0)
    c_ref[0] = c.T
    su_ref[0] = s_up.T
    sd_ref[0] = s_dn.T


def _rope_tables(positions):
    B = positions.shape[0]
    half = ROPE_DIMS // 2
    inv = ROPE_THETA ** (-jnp.arange(0, ROPE_DIMS, 2, dtype=F32) / ROPE_DIMS)
    table = jax.ShapeDtypeStruct((B, SEQ, LANES), F32)
    out_spec = pl.BlockSpec((1, SEQ, LANES), lambda b: (b, 0, 0))
    return pl.pallas_call(
        _rope_table_kernel,
        grid=(B,),
        in_specs=[pl.BlockSpec((1, 1, SEQ), lambda b: (b, 0, 0)), _resident((half, 1))],
        out_specs=[out_spec] * 3,
        out_shape=[table] * 3,
        compiler_params=_cparams("parallel"),
        name="rope_tables",
    )(positions.reshape(B, 1, SEQ), inv.reshape(half, 1))


def _pool_kernel(u_ref, w_ref, scale_ref, o_ref):
    u = u_ref[0]
    row = lax.broadcasted_iota(jnp.int32, u.shape, 0)
    lane = lax.broadcasted_iota(jnp.int32, u.shape, 1)

    def shifted(x, s):
        return jnp.where(row >= s, pltpu.roll(x, s, axis=0), 0.0)

    s2 = u + shifted(u, 1)
    s4 = s2 + shifted(s2, 2)
    s8 = s4 + shifted(s4, 4)
    s16 = s8 + shifted(s8, 8)
    grp = lane // POOL_GROUP
    wsum = jnp.where(grp == 0, s2, jnp.where(grp == 1, s4, jnp.where(grp == 2, s8, s16)))
    wnd = jnp.where(grp == 0, 2, jnp.where(grp == 1, 4, jnp.where(grp == 2, 8, 16)))
    cnt = jnp.minimum(row + 1, wnd).astype(F32)
    pooled = (wsum / cnt - u).astype(BF16)
    mixed = jnp.dot(pooled, w_ref[...], preferred_element_type=F32)
    o_ref[0] = (mixed * scale_ref[...]).astype(BF16)


def _pool(u, w_blockdiag, scale):
    B = u.shape[0]
    return pl.pallas_call(
        _pool_kernel,
        grid=(B,),
        in_specs=[
            pl.BlockSpec((1, SEQ, GROUP_W), lambda b: (b, 0, 0)),
            _resident((GROUP_W, GROUP_W)),
            _resident((1, GROUP_W)),
        ],
        out_specs=pl.BlockSpec((1, SEQ, GROUP_W), lambda b: (b, 0, 0)),
        out_shape=jax.ShapeDtypeStruct((B, SEQ, GROUP_W), BF16),
        compiler_params=_cparams("parallel"),
        name="pool",
    )(u, w_blockdiag, scale)


def _conv_kernel(u_ref, w_ref, b_ref, lg_ref, lb_ref, o_ref, hs_ref):
    a = u_ref[0, :, 0:GROUP_W]
    g = u_ref[0, :, GROUP_W:]
    hs_ref[0, 0:CONV_PAD, :] = jnp.zeros((CONV_PAD, GROUP_W), F32)
    hs_ref[0, CONV_PAD:, :] = a * jax.nn.sigmoid(g)
    for b in range(1, SUBLANES):
        hs_ref[b, 0:CONV_COPY_ROWS, :] = hs_ref[0, b:b + CONV_COPY_ROWS, :]
    first = CONV_PAD - (CONV_KERNEL - 1)

    def chunk(c, carry):
        r0 = pl.multiple_of(c * CONV_ROWS, CONV_ROWS)
        acc = None
        for j in range(CONV_KERNEL):
            a_, b_ = divmod(first + j, SUBLANES)
            term = w_ref[j:j + 1, :] * hs_ref[b_, pl.ds(r0 + SUBLANES * a_, CONV_ROWS), :]
            acc = term if acc is None else acc + term
        y = acc + b_ref[...]
        mu = jnp.mean(y, axis=-1, keepdims=True)
        var = jnp.mean(jnp.square(y - mu), axis=-1, keepdims=True)
        z = (y - mu) * lax.rsqrt(var + LN_EPS) * lg_ref[...] + lb_ref[...]
        o_ref[0, pl.ds(r0, CONV_ROWS), :] = jax.nn.silu(z).astype(BF16)
        return carry

    lax.fori_loop(0, SEQ // CONV_ROWS, chunk, 0, unroll=2)


def _conv(u, w, b, ln_g, ln_b):
    B = u.shape[0]
    return pl.pallas_call(
        _conv_kernel,
        grid=(B,),
        in_specs=[
            pl.BlockSpec((1, SEQ, 2 * GROUP_W), lambda b_: (b_, 0, 0)),
            _resident((CONV_KERNEL, GROUP_W)),
            _resident((1, GROUP_W)), _resident((1, GROUP_W)), _resident((1, GROUP_W)),
        ],
        out_specs=pl.BlockSpec((1, SEQ, GROUP_W), lambda b_: (b_, 0, 0)),
        out_shape=jax.ShapeDtypeStruct((B, SEQ, GROUP_W), BF16),
        scratch_shapes=[pltpu.VMEM((SUBLANES, SEQ + CONV_PAD, GROUP_W), F32)],
        compiler_params=_cparams("parallel"),
        name="conv",
    )(u, w, b, ln_g, ln_b)


def _scores_t(k, q):
    return lax.dot_general(k, q, (((1,), (1,)), ((), ())), preferred_element_type=F32)


class _AttnUnit:
    def __init__(self, n_chunks, score_fn, vt_fn, s_ref, done_fn):
        self.n_chunks, self.score_fn, self.vt_fn, self.s_ref, self.done_fn = n_chunks, score_fn, vt_fn, s_ref, done_fn
        self.m = None
        self.acc = None

    def score(self, j):
        s = self.score_fn(j)
        self.s_ref[j * KEY_CHUNK:(j + 1) * KEY_CHUNK, :] = s
        cm = s.max(axis=0, keepdims=True)
        self.m = cm if self.m is None else jnp.maximum(self.m, cm)

    def apply(self, j):
        p = jnp.exp2(self.s_ref[j * KEY_CHUNK:(j + 1) * KEY_CHUNK, :] - self.m).astype(BF16)
        pv = jnp.dot(self.vt_fn(j), p, preferred_element_type=F32)
        self.acc = pv if self.acc is None else self.acc + pv

    def finish(self):
        self.done_fn(self.acc[0:HEAD_DIM] / self.acc[HEAD_DIM:HEAD_DIM + 1])


def _score_slot(unit_index):
    stream, pos = unit_index % ATTN_STREAMS, unit_index // ATTN_STREAMS
    return 2 * stream + pos % 2


def _pipeline(units):
    prev = None
    for unit in list(units) + [None]:
        n_score = unit.n_chunks if unit is not None else 0
        n_apply = prev.n_chunks if prev is not None else 0
        for j in range(max(n_score, n_apply)):
            if j < n_score:
                unit.score(j)
            if j < n_apply:
                prev.apply(j)
            yield
        if prev is not None:
            prev.finish()
        prev = unit


def _run_interleaved(units, streams=1):
    live = [_pipeline(units[s::streams]) for s in range(streams)]
    while live:
        live = [g for g in live if next(g, StopIteration) is not StopIteration]


def _store_values_t(v_slab_ref_slice, vt_ref, pair):
    vt = v_slab_ref_slice.T
    ones_blk = (lax.broadcasted_iota(jnp.int32, (BF16_ROWS, SEQ), 0) == 0).astype(BF16)
    for hh in range(2):
        h = 2 * pair + hh
        vt_ref[h, 0:HEAD_DIM, :] = vt[hh * HEAD_DIM:(hh + 1) * HEAD_DIM].astype(BF16)
        vt_ref[h, HEAD_DIM:VT_ROWS, :] = ones_blk


def _store_heads(o_ref, r0, outs_t):
    for pair in range(2):
        both = jnp.concatenate(outs_t[2 * pair:2 * pair + 2], axis=0)
        o_ref[0, r0:r0 + Q_BLOCK, pair * LANES:(pair + 1) * LANES] = both.T.astype(BF16)


def _moba_kernel(qkv_ref, causal_ref, o_ref, qa_ref, ka_ref, vt_ref, s_ref):
    S = SEQ
    blk_row = lax.broadcasted_iota(jnp.int32, (MOBA_NB, S), 0)
    q_blk = lax.broadcasted_iota(jnp.int32, (MOBA_NB, S), 1) // MOBA_BLOCK
    past = blk_row < q_blk
    key_blk = lax.broadcasted_iota(jnp.int32, (S, LANES), 0) // MOBA_BLOCK
    onehot = (lax.broadcasted_iota(jnp.int32, (S, LANES), 1) == key_blk).astype(BF16)
    lane_head = lax.broadcasted_iota(jnp.int32, (1, LANES), 1) // HEAD_DIM

    def split(t):
        hi = t.astype(BF16)
        return hi, (t - hi.astype(F32)).astype(BF16)

    for pair in range(2):
        _store_values_t(qkv_ref[0, :, 2 * GROUP_W + pair * LANES: 2 * GROUP_W + (pair + 1) * LANES], vt_ref, pair)
        q_slab = qkv_ref[0, :, pair * LANES:(pair + 1) * LANES] * Q_SCALE
        k_slab = qkv_ref[0, :, GROUP_W + pair * LANES: GROUP_W + (pair + 1) * LANES]
        kmean = jnp.mean(k_slab.reshape(MOBA_NB, MOBA_BLOCK, LANES), axis=1)
        kmean2 = jnp.concatenate([jnp.where(lane_head == hh, kmean, 0.0) for hh in range(2)], axis=0)
        q_hi, q_lo = split(q_slab)
        k_hi, k_lo = split(kmean2)
        gate2 = _scores_t(jnp.concatenate([k_hi, k_lo, k_hi], axis=1),
                          jnp.concatenate([q_hi, q_hi, q_lo], axis=1))
        for hh in range(2):
            h = 2 * pair + hh
            gate = jnp.where(past, gate2[hh * MOBA_NB:(hh + 1) * MOBA_NB], NEG_INF)
            rank = jnp.zeros((MOBA_NB, S), jnp.int32)
            for jp in range(MOBA_NB):
                gj = gate[jp:jp + 1, :]
                beats = (gj > gate) | ((gj == gate) & (jp < blk_row))
                rank = rank + beats.astype(jnp.int32)
            keep = (past & (rank < MOBA_TOPK)) | (blk_row == q_blk)
            bias_t = jnp.where(keep, 0.0, NEG_INF)
            bias_t = jnp.concatenate([bias_t, jnp.zeros((LANES - MOBA_NB, S), F32)], axis=0)
            qa_ref[h] = jnp.concatenate([q_hi, bias_t.T.astype(BF16)], axis=1)
            ka_ref[h] = jnp.concatenate([jnp.where(lane_head == hh, k_slab, 0.0).astype(BF16), onehot], axis=1)

    def make_unit(i, h, outs):
        r0 = i * Q_BLOCK

        def score(j):
            s = _scores_t(ka_ref[h, j * KEY_CHUNK:(j + 1) * KEY_CHUNK, :], qa_ref[h, r0:r0 + Q_BLOCK, :])
            d = j - i * Q_CHUNKS
            return s + causal_ref[d * KEY_CHUNK:(d + 1) * KEY_CHUNK, :] if d >= 0 else s

        def done(out_t):
            outs.append(out_t)
            if len(outs) == N_HEADS:
                _store_heads(o_ref, r0, outs)

        return _AttnUnit((i + 1) * Q_CHUNKS, score, lambda j: vt_ref[h, :, j * KEY_CHUNK:(j + 1) * KEY_CHUNK],
                         s_ref.at[_score_slot(i * N_HEADS + h)], done)

    units = []
    for i in range(SEQ // Q_BLOCK):
        outs = []
        units += [make_unit(i, h, outs) for h in range(N_HEADS)]
    _run_interleaved(units, ATTN_STREAMS)


def _moba(qkv, causal_bias_t):
    B = qkv.shape[0]
    return pl.pallas_call(
        _moba_kernel,
        grid=(B,),
        in_specs=[
            pl.BlockSpec((1, SEQ, 3 * GROUP_W), lambda b: (b, 0, 0)),
            _resident((Q_BLOCK, Q_BLOCK)),
        ],
        out_specs=pl.BlockSpec((1, SEQ, GROUP_W), lambda b: (b, 0, 0)),
        out_shape=jax.ShapeDtypeStruct((B, SEQ, GROUP_W), BF16),
        scratch_shapes=[
            pltpu.VMEM((N_HEADS, SEQ, 2 * LANES), BF16),
            pltpu.VMEM((N_HEADS, SEQ, 2 * LANES), BF16),
            pltpu.VMEM((N_HEADS, VT_ROWS, SEQ), BF16),
            pltpu.VMEM((2 * ATTN_STREAMS, SEQ, Q_BLOCK), F32),
        ],
        compiler_params=_cparams("parallel"),
        name="moba",
    )(qkv, causal_bias_t)


def _dil_kernel(qkv_ref, bias_ref, o_ref, q_ref, k_ref, vt_ref, s_ref):
    for pair in range(2):
        _store_values_t(qkv_ref[0, :, 2 * GROUP_W + pair * LANES: 2 * GROUP_W + (pair + 1) * LANES], vt_ref, pair)
    lane_head = lax.broadcasted_iota(jnp.int32, (1, LANES), 1) // HEAD_DIM
    for pair in range(2):
        q_ref[pair] = (qkv_ref[0, :, pair * LANES:(pair + 1) * LANES] * Q_SCALE).astype(BF16)
        k_slab = qkv_ref[0, :, GROUP_W + pair * LANES: GROUP_W + (pair + 1) * LANES]
        for hh in range(2):
            k_ref[2 * pair + hh] = jnp.where(lane_head == hh, k_slab, 0.0).astype(BF16)

    def make_unit(i, h, outs):
        r0 = i * Q_BLOCK
        bias0 = SEQ - (r0 + Q_BLOCK)

        def score(j):
            keys = slice(j * KEY_CHUNK, (j + 1) * KEY_CHUNK)
            s = _scores_t(k_ref[h, keys, :], q_ref[h // 2, r0:r0 + Q_BLOCK, :])
            return s + bias_ref[bias0 + j * KEY_CHUNK: bias0 + (j + 1) * KEY_CHUNK, :]

        def done(out_t):
            outs.append(out_t)
            if len(outs) == N_HEADS:
                _store_heads(o_ref, r0, outs)

        return _AttnUnit((i + 1) * Q_CHUNKS, score, lambda j: vt_ref[h, :, j * KEY_CHUNK:(j + 1) * KEY_CHUNK],
                          s_ref.at[_score_slot(i * N_HEADS + h)], done)

    units = []
    for i in range(SEQ // Q_BLOCK):
        outs = []
        units += [make_unit(i, h, outs) for h in range(N_HEADS)]
    _run_interleaved(units, ATTN_STREAMS)


def _dil(qkv, dist_bias_t):
    B = qkv.shape[0]
    return pl.pallas_call(
        _dil_kernel,
        grid=(B,),
        in_specs=[
            pl.BlockSpec((1, SEQ, 3 * GROUP_W), lambda b: (b, 0, 0)),
            _resident((SEQ, Q_BLOCK)),
        ],
        out_specs=pl.BlockSpec((1, SEQ, GROUP_W), lambda b: (b, 0, 0)),
        out_shape=jax.ShapeDtypeStruct((B, SEQ, GROUP_W), BF16),
        scratch_shapes=[
            pltpu.VMEM((N_HEADS // 2, SEQ, LANES), BF16),
            pltpu.VMEM((N_HEADS, SEQ, LANES), BF16),
            pltpu.VMEM((N_HEADS, VT_ROWS, SEQ), BF16),
            pltpu.VMEM((2 * ATTN_STREAMS, SEQ, Q_BLOCK), F32),
        ],
        compiler_params=_cparams("parallel"),
        name="dilated",
    )(qkv, dist_bias_t)


def _causal_bias_t():
    r = np.arange(Q_BLOCK)
    return jnp.asarray(np.where(r[:, None] <= r[None, :], 0.0, NEG_INF), F32)


def _dilated_distance_bias_t():
    r = np.arange(Q_BLOCK)[None, :]
    x = np.arange(SEQ)[:, None]
    d = r - (x - (SEQ - Q_BLOCK))
    cnt = ((d >= 0) & (d <= 128)).astype(np.int64)
    cnt = cnt + ((d >= 0) & (d % 4 == 0) & (d <= 512))
    cnt = cnt + ((d >= 0) & (d % 16 == 0) & (d <= 2048))
    return jnp.asarray(np.where(cnt > 0, np.log2(np.maximum(cnt, 1)), NEG_INF), F32)


def _pool_blockdiag(pool_w):
    n = len(POOL_WINDOWS)
    eye = jnp.eye(n, dtype=pool_w.dtype)
    return jnp.einsum('gcd,gh->gchd', pool_w, eye).reshape(n * POOL_GROUP, n * POOL_GROUP)


def kernel(x, positions, ffn1_norm, ffn1_gate, ffn1_up, ffn1_down, mix_norm, w_in, pool_w, pool_scale,
           conv_w, conv_b, conv_ln_g, conv_ln_b, w_out, ffn2_norm, ffn2_gate, ffn2_up, ffn2_down, final_norm):
    B, S, D = x.shape
    assert (S, D) == (SEQ, D_MODEL)
    rope = [t.reshape(B * S, LANES) for t in _rope_tables(positions)]
    causal_t = _causal_bias_t()
    dist_bias_t = _dilated_distance_bias_t()
    row = lambda t: t.reshape(1, -1)
    rows = lambda t: t.reshape(DEPTH, 1, -1)
    bf = lambda t: t.astype(BF16)
    f1 = (rows(ffn1_norm), bf(ffn1_gate), bf(ffn1_up), bf(ffn1_down))
    f2 = (rows(ffn2_norm), bf(ffn2_gate), bf(ffn2_up), bf(ffn2_down))
    mix_g, w_in_b, w_out_b = rows(mix_norm), bf(w_in), bf(w_out)
    x2d = x.reshape(B * S, D)
    for l in range(DEPTH):
        x2d, u_pool, qkv_m, qkv_d, u_conv = _ffn_inproj(l, x2d, *f1, mix_g, w_in_b, *rope)
        seq = lambda t: t.reshape(B, S, -1)
        y_pool = _pool(seq(u_pool), _pool_blockdiag(pool_w[l]).astype(BF16), row(pool_scale[l]))
        y_moba = _moba(seq(qkv_m), causal_t)
        y_dil = _dil(seq(qkv_d), dist_bias_t)
        y_conv = _conv(seq(u_conv), conv_w[l], row(conv_b[l]), row(conv_ln_g[l]), row(conv_ln_b[l]))
        flat = lambda t: t.reshape(B * S, GROUP_W)
        x2d = _mix_ffn(l, x2d, flat(y_pool), flat(y_moba), flat(y_dil), flat(y_conv), w_out_b, *f2,
                       row(final_norm), final_norm=(l == DEPTH - 1))
    return x2d.reshape(B, S, D)
```

```python
import functools
import math

import numpy as np
import jax
import jax.numpy as jnp
from jax import lax
from jax.experimental import pallas as pl
from jax.experimental.pallas import tpu as pltpu

F32 = jnp.float32
BF16 = jnp.bfloat16

D_MODEL = 1024
SEQ = 2048
DEPTH = 2
HEAD_DIM = 64
N_HEADS = 4
GROUP_W = 256
POOL_WINDOWS = (2, 4, 8, 16)
POOL_GROUP = 64
MOBA_BLOCK = 256
MOBA_NB = SEQ // MOBA_BLOCK
MOBA_TOPK = 3
CONV_KERNEL = 31
ROPE_THETA = 500000.0
ROPE_DIMS = 16
D_FF = 2816
FF_CHUNK = 256
FFN_ROWS = 512
D_IN = 2304
RMS_EPS = 1e-6
LN_EPS = 1e-5
NEG_INF = -1e30
ATTN_SCALE = HEAD_DIM ** -0.5
LOG2E = math.log2(math.e)
Q_SCALE = ATTN_SCALE * LOG2E
Q_BLOCK = 256
KEY_CHUNK = 256
Q_CHUNKS = Q_BLOCK // KEY_CHUNK
ATTN_STREAMS = 4
LANES = 128
SUBLANES = 8
BF16_ROWS = 16
VT_ROWS = HEAD_DIM + BF16_ROWS
CONV_PAD = 32
CONV_ROWS = 128

VMEM_LIMIT = 56 * 1024 * 1024


def _cparams(*sem):
    return pltpu.CompilerParams(dimension_semantics=sem, vmem_limit_bytes=VMEM_LIMIT)


def _rms(x, g):
    return x * lax.rsqrt(jnp.mean(x * x, axis=-1, keepdims=True) + RMS_EPS) * g


def _resident(shape):
    return pl.BlockSpec(shape, lambda *_: (0,) * len(shape), pipeline_mode=pl.Buffered(1))


def _layer(shape, l):
    return pl.BlockSpec((None,) + tuple(shape), lambda *_: (l,) + (0,) * len(shape), pipeline_mode=pl.Buffered(1))


def _half_step_ffn(x, g_ref, wg_ref, wu_ref, wd_ref, hid_ref, side_work=()):
    xn = _rms(x, g_ref[...]).astype(BF16)
    side_work = list(side_work)
    for c in range(D_FF // FF_CHUNK):
        cols = slice(c * FF_CHUNK, (c + 1) * FF_CHUNK)
        gate = jnp.dot(xn, wg_ref[:, cols], preferred_element_type=F32)
        up = jnp.dot(xn, wu_ref[:, cols], preferred_element_type=F32)
        hid_ref[:, cols] = (jax.nn.silu(gate) * up).astype(BF16)
        if side_work:
            side_work.pop(0)()
    for piece in side_work:
        piece()
    return x + 0.5 * jnp.dot(hid_ref[...], wd_ref[...], preferred_element_type=F32)


def _conv_tile_pieces(u_ref, halo_fn, w_ref, b_ref, lg_ref, lb_ref, hs_ref, out_ref):
    tm = u_ref.shape[0]
    copy_rows = tm + CONV_PAD - SUBLANES
    first = CONV_PAD - (CONV_KERNEL - 1)

    def gate():
        hs_ref[0, 0:CONV_PAD, :] = halo_fn()
        hs_ref[0, CONV_PAD:, :] = u_ref[:, 0:GROUP_W] * jax.nn.sigmoid(u_ref[:, GROUP_W:])

    def copies(shifts):
        def piece():
            for b in shifts:
                hs_ref[b, 0:copy_rows, :] = hs_ref[0, b:b + copy_rows, :]
        return piece

    def chunk(c):
        def piece():
            r0 = c * CONV_ROWS
            acc = None
            for j in range(CONV_KERNEL):
                a_, b_ = divmod(first + j, SUBLANES)
                term = w_ref[j:j + 1, :] * hs_ref[b_, r0 + SUBLANES * a_: r0 + SUBLANES * a_ + CONV_ROWS, :]
                acc = term if acc is None else acc + term
            y = acc + b_ref[...]
            mu = jnp.mean(y, axis=-1, keepdims=True)
            var = jnp.mean(jnp.square(y - mu), axis=-1, keepdims=True)
            z = (y - mu) * lax.rsqrt(var + LN_EPS) * lg_ref[...] + lb_ref[...]
            out_ref[r0:r0 + CONV_ROWS, :] = jax.nn.silu(z).astype(BF16)
        return piece

    return [gate, copies((1, 2, 3, 4)), copies((5, 6, 7))] + [chunk(c) for c in range(tm // CONV_ROWS)]


def _rope(t, c, s_up, s_dn):
    half = ROPE_DIMS // 2
    return t * c + pltpu.roll(t, half, axis=1) * s_up + pltpu.roll(t, LANES - half, axis=1) * s_dn


def _ffn_inproj_kernel(x_ref, g_ref, wg_ref, wu_ref, wd_ref, gm_ref, wi_ref, c_ref, su_ref, sd_ref,
                       o_ref, pool_ref, moba_ref, dil_ref, conv_ref, hid_ref):
    x = _half_step_ffn(x_ref[...], g_ref, wg_ref, wu_ref, wd_ref, hid_ref)
    o_ref[...] = x
    h = jnp.dot(_rms(x, gm_ref[...]).astype(BF16), wi_ref[...], preferred_element_type=F32)
    c, su, sd = c_ref[...], su_ref[...], sd_ref[...]
    pool_ref[...] = h[:, 0:GROUP_W]
    for out_ref, off in ((moba_ref, GROUP_W), (dil_ref, 4 * GROUP_W)):
        for part in range(6):
            piece = h[:, off + part * LANES: off + (part + 1) * LANES]
            if part < 4:
                piece = _rope(piece, c, su, sd)
            out_ref[:, part * LANES:(part + 1) * LANES] = piece
    conv_ref[...] = h[:, 7 * GROUP_W:]


def _mix_ffn_kernel(x_ref, yp_ref, ym_ref, yd_ref, u0_ref, un_ref, uh_ref, cw_ref, cb_ref, clg_ref, clb_ref,
                    wo_ref, g_ref, wg_ref, wu_ref, wd_ref, fin_ref, o_ref, hid_ref, hs_ref, yc_ref, *, final_norm):
    i = pl.program_id(0)
    conv_refs = (cw_ref, cb_ref, clg_ref, clb_ref, hs_ref)

    @pl.when(i == 0)
    def _():
        zero_halo = lambda: jnp.zeros((CONV_PAD, GROUP_W), F32)
        for piece in _conv_tile_pieces(u0_ref, zero_halo, *conv_refs, yc_ref.at[0]):
            piece()

    def halo():
        h = uh_ref[:, 0:GROUP_W] * jax.nn.sigmoid(uh_ref[:, GROUP_W:])
        return jnp.where((i + 1) % (SEQ // FFN_ROWS) == 0, 0.0, h)

    next_conv = _conv_tile_pieces(un_ref, halo, *conv_refs, yc_ref.at[(i + 1) % 2])
    mix = jnp.concatenate([yp_ref[...], ym_ref[...], yd_ref[...], yc_ref[i % 2]], axis=1)
    x = x_ref[...] + jnp.dot(mix, wo_ref[...], preferred_element_type=F32)
    y = _half_step_ffn(x, g_ref, wg_ref, wu_ref, wd_ref, hid_ref, side_work=next_conv)
    if final_norm:
        y = _rms(y, fin_ref[...])
    o_ref[...] = y


def _ffn_weight_specs(l):
    return [_layer((1, D_MODEL), l), _layer((D_MODEL, D_FF), l), _layer((D_MODEL, D_FF), l),
            _layer((D_FF, D_MODEL), l)]


def _rows(width, tm):
    return pl.BlockSpec((tm, width), lambda i: (i, 0))


def _ffn_inproj(l, x2d, norm_g, wg, wu, wd, mix_g, w_in, rope_c, rope_su, rope_sd, *, tm=FFN_ROWS):
    n_tok = x2d.shape[0]
    widths = (D_MODEL, GROUP_W, 3 * GROUP_W, 3 * GROUP_W, 2 * GROUP_W)
    return pl.pallas_call(
        _ffn_inproj_kernel,
        grid=(n_tok // tm,),
        in_specs=[_rows(D_MODEL, tm)] + _ffn_weight_specs(l)
                 + [_layer((1, D_MODEL), l), _layer((D_MODEL, D_IN), l)] + [_rows(LANES, tm)] * 3,
        out_specs=[_rows(w, tm) for w in widths],
        out_shape=[jax.ShapeDtypeStruct((n_tok, w), F32) for w in widths],
        scratch_shapes=[pltpu.VMEM((tm, D_FF), BF16)],
        compiler_params=_cparams("parallel"),
        name="ffn_inproj",
    )(x2d, norm_g, wg, wu, wd, mix_g, w_in, rope_c, rope_su, rope_sd)


def _mix_ffn(l, x2d, yp, ym, yd, u_conv, conv_w, conv_b, conv_lg, conv_lb, w_out, norm_g, wg, wu, wd, fin_g,
             *, final_norm, tm=FFN_ROWS):
    n_tok = x2d.shape[0]
    n_tiles = n_tok // tm
    halo_blocks = tm // CONV_PAD
    u_first = pl.BlockSpec((tm, 2 * GROUP_W), lambda i: (0, 0))
    u_next = pl.BlockSpec((tm, 2 * GROUP_W), lambda i: (jnp.minimum(i + 1, n_tiles - 1), 0))
    u_halo = pl.BlockSpec((CONV_PAD, 2 * GROUP_W),
                          lambda i: (jnp.minimum((i + 1) * halo_blocks - 1, n_tiles * halo_blocks - 1), 0))
    return pl.pallas_call(
        functools.partial(_mix_ffn_kernel, final_norm=final_norm),
        grid=(n_tiles,),
        in_specs=[_rows(D_MODEL, tm)] + [_rows(GROUP_W, tm)] * 3 + [u_first, u_next, u_halo]
                 + [_layer((CONV_KERNEL, GROUP_W), l)] + [_layer((1, GROUP_W), l)] * 3
                 + [_layer((D_MODEL, D_MODEL), l)] + _ffn_weight_specs(l) + [_resident((1, D_MODEL))],
        out_specs=_rows(D_MODEL, tm),
        out_shape=jax.ShapeDtypeStruct((n_tok, D_MODEL), F32),
        scratch_shapes=[pltpu.VMEM((tm, D_FF), BF16),
                        pltpu.VMEM((SUBLANES, tm + CONV_PAD, GROUP_W), F32),
                        pltpu.VMEM((2, tm, GROUP_W), BF16)],
        compiler_params=_cparams("arbitrary"),
        name="mix_ffn",
    )(x2d, yp, ym, yd, u_conv, u_conv, u_conv, conv_w, conv_b, conv_lg, conv_lb, w_out, norm_g, wg, wu, wd, fin_g)


def _rope_table_kernel(pos_ref, inv_ref, c_ref, su_ref, sd_ref):
    half = ROPE_DIMS // 2
    ang = pos_ref[0].astype(F32) * inv_ref[...]
    cos, sin = jnp.cos(ang), jnp.sin(ang)
    zeros = lambda n: jnp.zeros((n, SEQ), F32)
    rest = HEAD_DIM - ROPE_DIMS
    c = jnp.concatenate([cos, cos, jnp.ones((rest, SEQ), F32)] * 2, axis=0)
    s_up = jnp.concatenate([zeros(half), sin, zeros(rest)] * 2, axis=0)
    s_dn = jnp.concatenate([-sin, zeros(half), zeros(rest)] * 2, axis=0)
    c_ref[0] = c.T
    su_ref[0] = s_up.T
    sd_ref[0] = s_dn.T


def _rope_tables(positions):
    B = positions.shape[0]
    half = ROPE_DIMS // 2
    inv = ROPE_THETA ** (-jnp.arange(0, ROPE_DIMS, 2, dtype=F32) / ROPE_DIMS)
    table = jax.ShapeDtypeStruct((B, SEQ, LANES), F32)
    out_spec = pl.BlockSpec((1, SEQ, LANES), lambda b: (b, 0, 0))
    return pl.pallas_call(
        _rope_table_kernel,
        grid=(B,),
        in_specs=[pl.BlockSpec((1, 1, SEQ), lambda b: (b, 0, 0)), _resident((half, 1))],
        out_specs=[out_spec] * 3,
        out_shape=[table] * 3,
        compiler_params=_cparams("parallel"),
        name="rope_tables",
    )(positions.reshape(B, 1, SEQ), inv.reshape(half, 1))


def _pool_kernel(u_ref, w_ref, scale_ref, o_ref):
    u = u_ref[0]
    row = lax.broadcasted_iota(jnp.int32, u.shape, 0)
    lane = lax.broadcasted_iota(jnp.int32, u.shape, 1)

    def shifted(x, s):
        return jnp.where(row >= s, pltpu.roll(x, s, axis=0), 0.0)

    s2 = u + shifted(u, 1)
    s4 = s2 + shifted(s2, 2)
    s8 = s4 + shifted(s4, 4)
    s16 = s8 + shifted(s8, 8)
    grp = lane // POOL_GROUP
    wsum = jnp.where(grp == 0, s2, jnp.where(grp == 1, s4, jnp.where(grp == 2, s8, s16)))
    wnd = jnp.where(grp == 0, 2, jnp.where(grp == 1, 4, jnp.where(grp == 2, 8, 16)))
    cnt = jnp.minimum(row + 1, wnd).astype(F32)
    pooled = (wsum / cnt - u).astype(BF16)
    mixed = jnp.dot(pooled, w_ref[...], preferred_element_type=F32)
    o_ref[0] = (mixed * scale_ref[...]).astype(BF16)


def _pool(u, w_blockdiag, scale):
    B = u.shape[0]
    return pl.pallas_call(
        _pool_kernel,
        grid=(B,),
        in_specs=[
            pl.BlockSpec((1, SEQ, GROUP_W), lambda b: (b, 0, 0)),
            _resident((GROUP_W, GROUP_W)),
            _resident((1, GROUP_W)),
        ],
        out_specs=pl.BlockSpec((1, SEQ, GROUP_W), lambda b: (b, 0, 0)),
        out_shape=jax.ShapeDtypeStruct((B, SEQ, GROUP_W), BF16),
        compiler_params=_cparams("parallel"),
        name="pool",
    )(u, w_blockdiag, scale)


def _scores_t(k, q):
    return lax.dot_general(k, q, (((1,), (1,)), ((), ())), preferred_element_type=F32)


class _AttnUnit:
    def __init__(self, n_chunks, score_fn, vt_fn, s_ref, done_fn):
        self.n_chunks, self.score_fn, self.vt_fn, self.s_ref, self.done_fn = n_chunks, score_fn, vt_fn, s_ref, done_fn
        self.m = None
        self.acc = None

    def score(self, j):
        s = self.score_fn(j)
        self.s_ref[j * KEY_CHUNK:(j + 1) * KEY_CHUNK, :] = s
        cm = s.max(axis=0, keepdims=True)
        self.m = cm if self.m is None else jnp.maximum(self.m, cm)

    def apply(self, j):
        p = jnp.exp2(self.s_ref[j * KEY_CHUNK:(j + 1) * KEY_CHUNK, :] - self.m).astype(BF16)
        pv = jnp.dot(self.vt_fn(j), p, preferred_element_type=F32)
        self.acc = pv if self.acc is None else self.acc + pv

    def finish(self):
        self.done_fn(self.acc[0:HEAD_DIM] / self.acc[HEAD_DIM:HEAD_DIM + 1])


def _score_slot(unit_index):
    stream, pos = unit_index % ATTN_STREAMS, unit_index // ATTN_STREAMS
    return 2 * stream + pos % 2


def _pipeline(units):
    prev = None
    for unit in list(units) + [None]:
        n_score = unit.n_chunks if unit is not None else 0
        n_apply = prev.n_chunks if prev is not None else 0
        for j in range(max(n_score, n_apply)):
            if j < n_score:
                unit.score(j)
            if j < n_apply:
                prev.apply(j)
            yield
        if prev is not None:
            prev.finish()
        prev = unit


def _run_interleaved(units, streams=1):
    live = [_pipeline(units[s::streams]) for s in range(streams)]
    while live:
        live = [g for g in live if next(g, StopIteration) is not StopIteration]


def _store_values_t(v_slab_ref_slice, vt_ref, pair):
    vt = v_slab_ref_slice.T
    ones_blk = (lax.broadcasted_iota(jnp.int32, (BF16_ROWS, SEQ), 0) == 0).astype(BF16)
    for hh in range(2):
        h = 2 * pair + hh
        vt_ref[h, 0:HEAD_DIM, :] = vt[hh * HEAD_DIM:(hh + 1) * HEAD_DIM].astype(BF16)
        vt_ref[h, HEAD_DIM:VT_ROWS, :] = ones_blk


def _store_heads(o_ref, r0, outs_t):
    for pair in range(2):
        both = jnp.concatenate(outs_t[2 * pair:2 * pair + 2], axis=0)
        o_ref[0, r0:r0 + Q_BLOCK, pair * LANES:(pair + 1) * LANES] = both.T.astype(BF16)


def _moba_kernel(qkv_ref, causal_ref, o_ref, qa_ref, ka_ref, vt_ref, s_ref):
    S = SEQ
    blk_row = lax.broadcasted_iota(jnp.int32, (MOBA_NB, S), 0)
    q_blk = lax.broadcasted_iota(jnp.int32, (MOBA_NB, S), 1) // MOBA_BLOCK
    past = blk_row < q_blk
    key_blk = lax.broadcasted_iota(jnp.int32, (S, LANES), 0) // MOBA_BLOCK
    onehot = (lax.broadcasted_iota(jnp.int32, (S, LANES), 1) == key_blk).astype(BF16)
    lane_head = lax.broadcasted_iota(jnp.int32, (1, LANES), 1) // HEAD_DIM

    def split(t):
        hi = t.astype(BF16)
        return hi, (t - hi.astype(F32)).astype(BF16)

    for pair in range(2):
        _store_values_t(qkv_ref[0, :, 2 * GROUP_W + pair * LANES: 2 * GROUP_W + (pair + 1) * LANES], vt_ref, pair)
        q_slab = qkv_ref[0, :, pair * LANES:(pair + 1) * LANES] * Q_SCALE
        k_slab = qkv_ref[0, :, GROUP_W + pair * LANES: GROUP_W + (pair + 1) * LANES]
        kmean = jnp.mean(k_slab.reshape(MOBA_NB, MOBA_BLOCK, LANES), axis=1)
        kmean2 = jnp.concatenate([jnp.where(lane_head == hh, kmean, 0.0) for hh in range(2)], axis=0)
        q_hi, q_lo = split(q_slab)
        k_hi, k_lo = split(kmean2)
        gate2 = _scores_t(jnp.concatenate([k_hi, k_lo, k_hi], axis=1),
                          jnp.concatenate([q_hi, q_hi, q_lo], axis=1))
        for hh in range(2):
            h = 2 * pair + hh
            gate = jnp.where(past, gate2[hh * MOBA_NB:(hh + 1) * MOBA_NB], NEG_INF)
            rank = jnp.zeros((MOBA_NB, S), jnp.int32)
            for jp in range(MOBA_NB):
                gj = gate[jp:jp + 1, :]
                beats = (gj > gate) | ((gj == gate) & (jp < blk_row))
                rank = rank + beats.astype(jnp.int32)
            keep = (past & (rank < MOBA_TOPK)) | (blk_row == q_blk)
            bias_t = jnp.where(keep, 0.0, NEG_INF)
            bias_t = jnp.concatenate([bias_t, jnp.zeros((LANES - MOBA_NB, S), F32)], axis=0)
            qa_ref[h] = jnp.concatenate([q_hi, bias_t.T.astype(BF16)], axis=1)
            ka_ref[h] = jnp.concatenate([jnp.where(lane_head == hh, k_slab, 0.0).astype(BF16), onehot], axis=1)

    def make_unit(i, h, outs):
        r0 = i * Q_BLOCK

        def score(j):
            s = _scores_t(ka_ref[h, j * KEY_CHUNK:(j + 1) * KEY_CHUNK, :], qa_ref[h, r0:r0 + Q_BLOCK, :])
            d = j - i * Q_CHUNKS
            return s + causal_ref[d * KEY_CHUNK:(d + 1) * KEY_CHUNK, :] if d >= 0 else s

        def done(out_t):
            outs.append(out_t)
            if len(outs) == N_HEADS:
                _store_heads(o_ref, r0, outs)

        return _AttnUnit((i + 1) * Q_CHUNKS, score, lambda j: vt_ref[h, :, j * KEY_CHUNK:(j + 1) * KEY_CHUNK],
                         s_ref.at[_score_slot(i * N_HEADS + h)], done)

    units = []
    for i in range(SEQ // Q_BLOCK):
        outs = []
        units += [make_unit(i, h, outs) for h in range(N_HEADS)]
    _run_interleaved(units, ATTN_STREAMS)


def _moba(qkv, causal_bias_t):
    B = qkv.shape[0]
    return pl.pallas_call(
        _moba_kernel,
        grid=(B,),
        in_specs=[
            pl.BlockSpec((1, SEQ, 3 * GROUP_W), lambda b: (b, 0, 0)),
            _resident((Q_BLOCK, Q_BLOCK)),
        ],
        out_specs=pl.BlockSpec((1, SEQ, GROUP_W), lambda b: (b, 0, 0)),
        out_shape=jax.ShapeDtypeStruct((B, SEQ, GROUP_W), BF16),
        scratch_shapes=[
            pltpu.VMEM((N_HEADS, SEQ, 2 * LANES), BF16),
            pltpu.VMEM((N_HEADS, SEQ, 2 * LANES), BF16),
            pltpu.VMEM((N_HEADS, VT_ROWS, SEQ), BF16),
            pltpu.VMEM((2 * ATTN_STREAMS, SEQ, Q_BLOCK), F32),
        ],
        compiler_params=_cparams("parallel"),
        name="moba",
    )(qkv, causal_bias_t)


def _dil_kernel(qkv_ref, bias_ref, o_ref, q_ref, k_ref, vt_ref, s_ref):
    for pair in range(2):
        _store_values_t(qkv_ref[0, :, 2 * GROUP_W + pair * LANES: 2 * GROUP_W + (pair + 1) * LANES], vt_ref, pair)
    lane_head = lax.broadcasted_iota(jnp.int32, (1, LANES), 1) // HEAD_DIM
    for pair in range(2):
        q_ref[pair] = (qkv_ref[0, :, pair * LANES:(pair + 1) * LANES] * Q_SCALE).astype(BF16)
        k_slab = qkv_ref[0, :, GROUP_W + pair * LANES: GROUP_W + (pair + 1) * LANES]
        for hh in range(2):
            k_ref[2 * pair + hh] = jnp.where(lane_head == hh, k_slab, 0.0).astype(BF16)

    def make_unit(i, h, outs):
        r0 = i * Q_BLOCK
        bias0 = SEQ - (r0 + Q_BLOCK)

        def score(j):
            keys = slice(j * KEY_CHUNK, (j + 1) * KEY_CHUNK)
            s = _scores_t(k_ref[h, keys, :], q_ref[h // 2, r0:r0 + Q_BLOCK, :])
            return s + bias_ref[bias0 + j * KEY_CHUNK: bias0 + (j + 1) * KEY_CHUNK, :]

        def done(out_t):
            outs.append(out_t)
            if len(outs) == N_HEADS:
                _store_heads(o_ref, r0, outs)

        return _AttnUnit((i + 1) * Q_CHUNKS, score, lambda j: vt_ref[h, :, j * KEY_CHUNK:(j + 1) * KEY_CHUNK],
                          s_ref.at[_score_slot(i * N_HEADS + h)], done)

    units = []
    for i in range(SEQ // Q_BLOCK):
        outs = []
        units += [make_unit(i, h, outs) for h in range(N_HEADS)]
    _run_interleaved(units, ATTN_STREAMS)


def _dil(qkv, dist_bias_t):
    B = qkv.shape[0]
    return pl.pallas_call(
        _dil_kernel,
        grid=(B,),
        in_specs=[
            pl.BlockSpec((1, SEQ, 3 * GROUP_W), lambda b: (b, 0, 0)),
            _resident((SEQ, Q_BLOCK)),
        ],
        out_specs=pl.BlockSpec((1, SEQ, GROUP_W), lambda b: (b, 0, 0)),
        out_shape=jax.ShapeDtypeStruct((B, SEQ, GROUP_W), BF16),
        scratch_shapes=[
            pltpu.VMEM((N_HEADS // 2, SEQ, LANES), BF16),
            pltpu.VMEM((N_HEADS, SEQ, LANES), BF16),
            pltpu.VMEM((N_HEADS, VT_ROWS, SEQ), BF16),
            pltpu.VMEM((2 * ATTN_STREAMS, SEQ, Q_BLOCK), F32),
        ],
        compiler_params=_cparams("parallel"),
        name="dilated",
    )(qkv, dist_bias_t)


def _causal_bias_t():
    r = np.arange(Q_BLOCK)
    return jnp.asarray(np.where(r[:, None] <= r[None, :], 0.0, NEG_INF), F32)


def _dilated_distance_bias_t():
    r = np.arange(Q_BLOCK)[None, :]
    x = np.arange(SEQ)[:, None]
    d = r - (x - (SEQ - Q_BLOCK))
    cnt = ((d >= 0) & (d <= 128)).astype(np.int64)
    cnt = cnt + ((d >= 0) & (d % 4 == 0) & (d <= 512))
    cnt = cnt + ((d >= 0) & (d % 16 == 0) & (d <= 2048))
    return jnp.asarray(np.where(cnt > 0, np.log2(np.maximum(cnt, 1)), NEG_INF), F32)


def _pool_blockdiag(pool_w):
    n = len(POOL_WINDOWS)
    eye = jnp.eye(n, dtype=pool_w.dtype)
    return jnp.einsum('gcd,gh->gchd', pool_w, eye).reshape(n * POOL_GROUP, n * POOL_GROUP)


def kernel(x, positions, ffn1_norm, ffn1_gate, ffn1_up, ffn1_down, mix_norm, w_in, pool_w, pool_scale,
           conv_w, conv_b, conv_ln_g, conv_ln_b, w_out, ffn2_norm, ffn2_gate, ffn2_up, ffn2_down, final_norm):
    B, S, D = x.shape
    assert (S, D) == (SEQ, D_MODEL)
    rope = [t.reshape(B * S, LANES) for t in _rope_tables(positions)]
    causal_t = _causal_bias_t()
    dist_bias_t = _dilated_distance_bias_t()
    row = lambda t: t.reshape(1, -1)
    rows = lambda t: t.reshape(DEPTH, 1, -1)
    bf = lambda t: t.astype(BF16)
    f1 = (rows(ffn1_norm), bf(ffn1_gate), bf(ffn1_up), bf(ffn1_down))
    f2 = (rows(ffn2_norm), bf(ffn2_gate), bf(ffn2_up), bf(ffn2_down))
    mix_g, w_in_b, w_out_b = rows(mix_norm), bf(w_in), bf(w_out)
    conv_p = (conv_w, rows(conv_b), rows(conv_ln_g), rows(conv_ln_b))
    x2d = x.reshape(B * S, D)
    for l in range(DEPTH):
        x2d, u_pool, qkv_m, qkv_d, u_conv = _ffn_inproj(l, x2d, *f1, mix_g, w_in_b, *rope)
        seq = lambda t: t.reshape(B, S, -1)
        y_pool = _pool(seq(u_pool), _pool_blockdiag(pool_w[l]).astype(BF16), row(pool_scale[l]))
        y_moba = _moba(seq(qkv_m), causal_t)
        y_dil = _dil(seq(qkv_d), dist_bias_t)
        flat = lambda t: t.reshape(B * S, GROUP_W)
        x2d = _mix_ffn(l, x2d, flat(y_pool), flat(y_moba), flat(y_dil), u_conv, *conv_p, w_out_b, *f2,
                       row(final_norm), final_norm=(l == DEPTH - 1))
    return x2d.reshape(B, S, D)
```

```python
import functools
import math

import numpy as np
import jax
import jax.numpy as jnp
from jax import lax
from jax.experimental import pallas as pl
from jax.experimental.pallas import tpu as pltpu

F32 = jnp.float32
BF16 = jnp.bfloat16

D_MODEL = 1024
SEQ = 2048
DEPTH = 2
HEAD_DIM = 64
N_HEADS = 4
GROUP_W = 256
POOL_WINDOWS = (2, 4, 8, 16)
POOL_GROUP = 64
MOBA_BLOCK = 256
MOBA_NB = SEQ // MOBA_BLOCK
MOBA_TOPK = 3
CONV_KERNEL = 31
ROPE_THETA = 500000.0
ROPE_DIMS = 16
D_FF = 2816
FF_CHUNK = 256
FFN_ROWS = 512
D_IN = 2304
RMS_EPS = 1e-6
LN_EPS = 1e-5
NEG_INF = -1e30
ATTN_SCALE = HEAD_DIM ** -0.5
LOG2E = math.log2(math.e)
Q_SCALE = ATTN_SCALE * LOG2E
Q_BLOCK = 256
KEY_CHUNK = 256
Q_CHUNKS = Q_BLOCK // KEY_CHUNK
ATTN_STREAMS = 4
LANES = 128
SUBLANES = 8
BF16_ROWS = 16
VT_ROWS = HEAD_DIM + BF16_ROWS
CONV_PAD = 32
CONV_ROWS = 64

VMEM_LIMIT = 56 * 1024 * 1024


def _cparams(*sem):
    return pltpu.CompilerParams(dimension_semantics=sem, vmem_limit_bytes=VMEM_LIMIT)


def _rms(x, g):
    return x * lax.rsqrt(jnp.mean(x * x, axis=-1, keepdims=True) + RMS_EPS) * g


def _resident(shape):
    return pl.BlockSpec(shape, lambda *_: (0,) * len(shape), pipeline_mode=pl.Buffered(1))


def _layer(shape, l):
    return pl.BlockSpec((None,) + tuple(shape), lambda *_: (l,) + (0,) * len(shape), pipeline_mode=pl.Buffered(1))


def _zero_token(t):
    rows, cols = t.shape
    return jnp.minimum(jnp.abs(t.reshape(rows // SUBLANES, SUBLANES, cols)).max(axis=0), 0.0)


def _half_step_ffn(x, g_ref, wg_ref, wu_ref, wd_ref, hid_ref, side_work=()):
    n_up, n_down = D_FF // FF_CHUNK, D_MODEL // FF_CHUNK
    after = [[] for _ in range(n_up + n_down - 1)]
    for k, piece in enumerate(side_work):
        after[k * len(after) // len(side_work)].append(piece)

    def run_side_work(slot):
        token = None
        for piece in (after[slot] if slot < len(after) else ()):
            t = piece()
            token = t if token is None else token + t
        return token

    def gated(value, token):
        return value if token is None else jnp.concatenate([value[0:SUBLANES] + token, value[SUBLANES:]], axis=0)

    xn = _rms(x, g_ref[...]).astype(BF16)
    token = None
    for c in range(n_up):
        cols = slice(c * FF_CHUNK, (c + 1) * FF_CHUNK)
        gate = jnp.dot(xn, wg_ref[:, cols], preferred_element_type=F32)
        up = jnp.dot(xn, wu_ref[:, cols], preferred_element_type=F32)
        hid_ref[:, cols] = gated(jax.nn.silu(gate) * up, token).astype(BF16)
        token = run_side_work(c)
    outs = []
    for c in range(n_down):
        cols = slice(c * FF_CHUNK, (c + 1) * FF_CHUNK)
        y = x[:, cols] + 0.5 * jnp.dot(hid_ref[...], wd_ref[:, cols], preferred_element_type=F32)
        outs.append(gated(y, token))
        token = run_side_work(n_up + c)
    return jnp.concatenate(outs, axis=1)


def _conv_tile_pieces(u_ref, halo_fn, w_ref, b_ref, lg_ref, lb_ref, hs_ref, out_ref):
    tm = u_ref.shape[0]
    copy_rows = tm + CONV_PAD - SUBLANES
    first = CONV_PAD - (CONV_KERNEL - 1)

    def gate():
        h = u_ref[:, 0:GROUP_W] * jax.nn.sigmoid(u_ref[:, GROUP_W:])
        hs_ref[0, 0:CONV_PAD, :] = halo_fn()
        hs_ref[0, CONV_PAD:, :] = h
        return _zero_token(h)

    def copies(shifts):
        def piece():
            token = None
            for b in shifts:
                v = hs_ref[0, b:b + copy_rows, :]
                hs_ref[b, 0:copy_rows, :] = v
                token = _zero_token(v) if token is None else token + _zero_token(v)
            return token
        return piece

    def chunk(c):
        def piece():
            r0 = c * CONV_ROWS
            acc = None
            for j in range(CONV_KERNEL):
                a_, b_ = divmod(first + j, SUBLANES)
                term = w_ref[j:j + 1, :] * hs_ref[b_, r0 + SUBLANES * a_: r0 + SUBLANES * a_ + CONV_ROWS, :]
                acc = term if acc is None else acc + term
            y = acc + b_ref[...]
            mu = jnp.mean(y, axis=-1, keepdims=True)
            var = jnp.mean(jnp.square(y - mu), axis=-1, keepdims=True)
            z = (y - mu) * lax.rsqrt(var + LN_EPS) * lg_ref[...] + lb_ref[...]
            out = jax.nn.silu(z)
            out_ref[r0:r0 + CONV_ROWS, :] = out.astype(BF16)
            return _zero_token(out)
        return piece

    return [gate, copies((1, 2, 3, 4)), copies((5, 6, 7))] + [chunk(c) for c in range(tm // CONV_ROWS)]


def _rope(t, c, s_up, s_dn):
    half = ROPE_DIMS // 2
    return t * c + pltpu.roll(t, half, axis=1) * s_up + pltpu.roll(t, LANES - half, axis=1) * s_dn


def _cast_side_outputs(cast_in, cast_out):
    for src, dst in zip(cast_in, cast_out):
        dst[...] = src[...].astype(BF16)


def _ffn_inproj_kernel(x_ref, g_ref, wg_ref, wu_ref, wd_ref, gm_ref, wi_ref, c_ref, su_ref, sd_ref, *rest, n_cast):
    cast_in, (o_ref, pool_ref, moba_ref, dil_ref, conv_ref) = rest[:n_cast], rest[n_cast:n_cast + 5]
    cast_out, (hid_ref,) = rest[n_cast + 5:2 * n_cast + 5], rest[2 * n_cast + 5:]
    _cast_side_outputs(cast_in, cast_out)
    x = _half_step_ffn(x_ref[...], g_ref, wg_ref, wu_ref, wd_ref, hid_ref)
    o_ref[...] = x
    h = jnp.dot(_rms(x, gm_ref[...]).astype(BF16), wi_ref[...], preferred_element_type=F32)
    c, su, sd = c_ref[...], su_ref[...], sd_ref[...]
    pool_ref[...] = h[:, 0:GROUP_W]
    for out_ref, off in ((moba_ref, GROUP_W), (dil_ref, 4 * GROUP_W)):
        for part in range(6):
            piece = h[:, off + part * LANES: off + (part + 1) * LANES]
            if part < 4:
                piece = _rope(piece, c, su, sd)
            out_ref[:, part * LANES:(part + 1) * LANES] = piece
    conv_ref[...] = h[:, 7 * GROUP_W:]


def _mix_ffn_kernel(x_ref, yp_ref, ym_ref, yd_ref, u0_ref, un_ref, uh_ref, cw_ref, cb_ref, clg_ref, clb_ref,
                    wo_ref, g_ref, wg_ref, wu_ref, wd_ref, fin_ref, *rest, n_cast, final_norm):
    cast_in, o_ref, cast_out = rest[:n_cast], rest[n_cast], rest[n_cast + 1:2 * n_cast + 1]
    hid_ref, hs_ref, yc_ref = rest[2 * n_cast + 1:]
    _cast_side_outputs(cast_in, cast_out)
    i = pl.program_id(0)
    conv_refs = (cw_ref, cb_ref, clg_ref, clb_ref, hs_ref)

    @pl.when(i == 0)
    def _():
        zero_halo = lambda: jnp.zeros((CONV_PAD, GROUP_W), F32)
        for piece in _conv_tile_pieces(u0_ref, zero_halo, *conv_refs, yc_ref.at[0]):
            piece()

    def halo():
        h = uh_ref[:, 0:GROUP_W] * jax.nn.sigmoid(uh_ref[:, GROUP_W:])
        return jnp.where((i + 1) % (SEQ // FFN_ROWS) == 0, 0.0, h)

    next_conv = _conv_tile_pieces(un_ref, halo, *conv_refs, yc_ref.at[(i + 1) % 2])
    mix = jnp.concatenate([yp_ref[...], ym_ref[...], yd_ref[...], yc_ref[i % 2]], axis=1)
    x = x_ref[...] + jnp.dot(mix, wo_ref[...], preferred_element_type=F32)
    y = _half_step_ffn(x, g_ref, wg_ref, wu_ref, wd_ref, hid_ref, side_work=next_conv)
    if final_norm:
        y = _rms(y, fin_ref[...])
    o_ref[...] = y


def _ffn_weight_specs(l):
    return [_layer((1, D_MODEL), l), _resident((D_MODEL, D_FF)), _resident((D_MODEL, D_FF)),
            _resident((D_FF, D_MODEL))]


def _rows(width, tm):
    return pl.BlockSpec((tm, width), lambda i: (i, 0))


def _cast_specs(cast, n_steps):
    in_specs, out_specs, out_shapes, operands = [], [], [], []
    weights, layer = cast if cast is not None else ((), 0)
    for w in weights:
        _, rows, cols = w.shape
        rb = next(r for r in range(BF16_ROWS, rows + 1, BF16_ROWS) if rows % r == 0 and rows // r <= n_steps)
        last = rows // rb - 1
        in_specs.append(pl.BlockSpec((None, rb, cols), lambda i, last=last: (layer, jnp.minimum(i, last), 0)))
        out_specs.append(pl.BlockSpec((rb, cols), lambda i, last=last: (jnp.minimum(i, last), 0)))
        out_shapes.append(jax.ShapeDtypeStruct((rows, cols), BF16))
        operands.append(w)
    return in_specs, out_specs, out_shapes, operands


def _ffn_inproj(l, x2d, norm_g, wg, wu, wd, mix_g, w_in, rope_c, rope_su, rope_sd, *, cast=None, tm=FFN_ROWS):
    n_tok = x2d.shape[0]
    widths = (D_MODEL, GROUP_W, 3 * GROUP_W, 3 * GROUP_W, 2 * GROUP_W)
    c_in, c_out, c_shapes, c_ops = _cast_specs(cast, n_tok // tm)
    return pl.pallas_call(
        functools.partial(_ffn_inproj_kernel, n_cast=len(c_ops)),
        grid=(n_tok // tm,),
        in_specs=[_rows(D_MODEL, tm)] + _ffn_weight_specs(l)
                 + [_layer((1, D_MODEL), l), _resident((D_MODEL, D_IN))] + [_rows(LANES, tm)] * 3 + c_in,
        out_specs=[_rows(w, tm) for w in widths] + c_out,
        out_shape=[jax.ShapeDtypeStruct((n_tok, w), F32) for w in widths] + c_shapes,
        scratch_shapes=[pltpu.VMEM((tm, D_FF), BF16)],
        compiler_params=_cparams("arbitrary"),
        name="ffn_inproj",
    )(x2d, norm_g, wg, wu, wd, mix_g, w_in, rope_c, rope_su, rope_sd, *c_ops)


def _mix_ffn(l, x2d, yp, ym, yd, u_conv, conv_w, conv_b, conv_lg, conv_lb, w_out, norm_g, wg, wu, wd, fin_g,
             *, final_norm, cast=None, tm=FFN_ROWS):
    n_tok = x2d.shape[0]
    n_tiles = n_tok // tm
    halo_blocks = tm // CONV_PAD
    u_first = pl.BlockSpec((tm, 2 * GROUP_W), lambda i: (0, 0))
    u_next = pl.BlockSpec((tm, 2 * GROUP_W), lambda i: (jnp.minimum(i + 1, n_tiles - 1), 0))
    u_halo = pl.BlockSpec((CONV_PAD, 2 * GROUP_W),
                          lambda i: (jnp.minimum((i + 1) * halo_blocks - 1, n_tiles * halo_blocks - 1), 0))
    c_in, c_out, c_shapes, c_ops = _cast_specs(cast, n_tiles)
    return pl.pallas_call(
        functools.partial(_mix_ffn_kernel, n_cast=len(c_ops), final_norm=final_norm),
        grid=(n_tiles,),
        in_specs=[_rows(D_MODEL, tm)] + [_rows(GROUP_W, tm)] * 3 + [u_first, u_next, u_halo]
                 + [_layer((CONV_KERNEL, GROUP_W), l)] + [_layer((1, GROUP_W), l)] * 3
                 + [_resident((D_MODEL, D_MODEL))] + _ffn_weight_specs(l) + [_resident((1, D_MODEL))] + c_in,
        out_specs=[_rows(D_MODEL, tm)] + c_out,
        out_shape=[jax.ShapeDtypeStruct((n_tok, D_MODEL), F32)] + c_shapes,
        scratch_shapes=[pltpu.VMEM((tm, D_FF), BF16),
                        pltpu.VMEM((SUBLANES, tm + CONV_PAD, GROUP_W), F32),
                        pltpu.VMEM((2, tm, GROUP_W), BF16)],
        compiler_params=_cparams("arbitrary"),
        name="mix_ffn",
    )(x2d, yp, ym, yd, u_conv, u_conv, u_conv, conv_w, conv_b, conv_lg, conv_lb, w_out, norm_g, wg, wu, wd, fin_g,
      *c_ops)


def _rope_table_kernel(pos_ref, inv_ref, c_ref, su_ref, sd_ref):
    half = ROPE_DIMS // 2
    ang = pos_ref[0].astype(F32) * inv_ref[...]
    cos, sin = jnp.cos(ang), jnp.sin(ang)
    zeros = lambda n: jnp.zeros((n, SEQ), F32)
    rest = HEAD_DIM - ROPE_DIMS
    c = jnp.concatenate([cos, cos, jnp.ones((rest, SEQ), F32)] * 2, axis=0)
    s_up = jnp.concatenate([zeros(half), sin, zeros(rest)] * 2, axis=0)
    s_dn = jnp.concatenate([-sin, zeros(half), zeros(rest)] * 2, axis=0)
    c_ref[0] = c.T
    su_ref[0] = s_up.T
    sd_ref[0] = s_dn.T


def _rope_tables(positions):
    B = positions.shape[0]
    half = ROPE_DIMS // 2
    inv = ROPE_THETA ** (-jnp.arange(0, ROPE_DIMS, 2, dtype=F32) / ROPE_DIMS)
    table = jax.ShapeDtypeStruct((B, SEQ, LANES), F32)
    out_spec = pl.BlockSpec((1, SEQ, LANES), lambda b: (b, 0, 0))
    return pl.pallas_call(
        _rope_table_kernel,
        grid=(B,),
        in_specs=[pl.BlockSpec((1, 1, SEQ), lambda b: (b, 0, 0)), _resident((half, 1))],
        out_specs=[out_spec] * 3,
        out_shape=[table] * 3,
        compiler_params=_cparams("parallel"),
        name="rope_tables",
    )(positions.reshape(B, 1, SEQ), inv.reshape(half, 1))


def _pool_kernel(u_ref, w_ref, scale_ref, o_ref):
    u = u_ref[0]
    row = lax.broadcasted_iota(jnp.int32, u.shape, 0)
    lane = lax.broadcasted_iota(jnp.int32, u.shape, 1)

    def shifted(x, s):
        return jnp.where(row >= s, pltpu.roll(x, s, axis=0), 0.0)

    s2 = u + shifted(u, 1)
    s4 = s2 + shifted(s2, 2)
    s8 = s4 + shifted(s4, 4)
    s16 = s8 + shifted(s8, 8)
    grp = lane // POOL_GROUP
    wsum = jnp.where(grp == 0, s2, jnp.where(grp == 1, s4, jnp.where(grp == 2, s8, s16)))
    wnd = jnp.where(grp == 0, 2, jnp.where(grp == 1, 4, jnp.where(grp == 2, 8, 16)))
    cnt = jnp.minimum(row + 1, wnd).astype(F32)
    pooled = (wsum / cnt - u).astype(BF16)
    mixed = jnp.dot(pooled, w_ref[...], preferred_element_type=F32)
    o_ref[0] = (mixed * scale_ref[...]).astype(BF16)


def _pool(u, w_blockdiag, scale):
    B = u.shape[0]
    return pl.pallas_call(
        _pool_kernel,
        grid=(B,),
        in_specs=[
            pl.BlockSpec((1, SEQ, GROUP_W), lambda b: (b, 0, 0)),
            _resident((GROUP_W, GROUP_W)),
            _resident((1, GROUP_W)),
        ],
        out_specs=pl.BlockSpec((1, SEQ, GROUP_W), lambda b: (b, 0, 0)),
        out_shape=jax.ShapeDtypeStruct((B, SEQ, GROUP_W), BF16),
        compiler_params=_cparams("parallel"),
        name="pool",
    )(u, w_blockdiag, scale)


def _scores_t(k, q):
    return lax.dot_general(k, q, (((1,), (1,)), ((), ())), preferred_element_type=F32)


class _AttnUnit:
    def __init__(self, n_chunks, score_fn, vt_fn, s_ref, done_fn):
        self.n_chunks, self.score_fn, self.vt_fn, self.s_ref, self.done_fn = n_chunks, score_fn, vt_fn, s_ref, done_fn
        self.m = None
        self.acc = None

    def score(self, j):
        s = self.score_fn(j)
        self.s_ref[j * KEY_CHUNK:(j + 1) * KEY_CHUNK, :] = s
        cm = s.max(axis=0, keepdims=True)
        self.m = cm if self.m is None else jnp.maximum(self.m, cm)

    def apply(self, j):
        p = jnp.exp2(self.s_ref[j * KEY_CHUNK:(j + 1) * KEY_CHUNK, :] - self.m).astype(BF16)
        pv = jnp.dot(self.vt_fn(j), p, preferred_element_type=F32)
        self.acc = pv if self.acc is None else self.acc + pv

    def finish(self):
        self.done_fn(self.acc[0:HEAD_DIM] / self.acc[HEAD_DIM:HEAD_DIM + 1])


def _score_slot(unit_index):
    stream, pos = unit_index % ATTN_STREAMS, unit_index // ATTN_STREAMS
    return 2 * stream + pos % 2


def _pipeline(units):
    prev = None
    for unit in list(units) + [None]:
        n_score = unit.n_chunks if unit is not None else 0
        n_apply = prev.n_chunks if prev is not None else 0
        for j in range(max(n_score, n_apply)):
            if j < n_score:
                unit.score(j)
            if j < n_apply:
                prev.apply(j)
            yield
        if prev is not None:
            prev.finish()
        prev = unit


def _run_interleaved(units, streams=1):
    live = [_pipeline(units[s::streams]) for s in range(streams)]
    while live:
        live = [g for g in live if next(g, StopIteration) is not StopIteration]


def _store_values_t(v_slab_ref_slice, vt_ref, pair):
    vt = v_slab_ref_slice.T
    ones_blk = (lax.broadcasted_iota(jnp.int32, (BF16_ROWS, SEQ), 0) == 0).astype(BF16)
    for hh in range(2):
        h = 2 * pair + hh
        vt_ref[h, 0:HEAD_DIM, :] = vt[hh * HEAD_DIM:(hh + 1) * HEAD_DIM].astype(BF16)
        vt_ref[h, HEAD_DIM:VT_ROWS, :] = ones_blk


def _store_heads(o_ref, r0, outs_t):
    for pair in range(2):
        both = jnp.concatenate(outs_t[2 * pair:2 * pair + 2], axis=0)
        o_ref[0, r0:r0 + Q_BLOCK, pair * LANES:(pair + 1) * LANES] = both.T.astype(BF16)


def _moba_kernel(qkv_ref, causal_ref, o_ref, qa_ref, ka_ref, vt_ref, s_ref):
    S = SEQ
    blk_row = lax.broadcasted_iota(jnp.int32, (MOBA_NB, S), 0)
    q_blk = lax.broadcasted_iota(jnp.int32, (MOBA_NB, S), 1) // MOBA_BLOCK
    past = blk_row < q_blk
    key_blk = lax.broadcasted_iota(jnp.int32, (S, LANES), 0) // MOBA_BLOCK
    onehot = (lax.broadcasted_iota(jnp.int32, (S, LANES), 1) == key_blk).astype(BF16)
    lane_head = lax.broadcasted_iota(jnp.int32, (1, LANES), 1) // HEAD_DIM

    def split(t):
        hi = t.astype(BF16)
        return hi, (t - hi.astype(F32)).astype(BF16)

    for pair in range(2):
        _store_values_t(qkv_ref[0, :, 2 * GROUP_W + pair * LANES: 2 * GROUP_W + (pair + 1) * LANES], vt_ref, pair)
        q_slab = qkv_ref[0, :, pair * LANES:(pair + 1) * LANES] * Q_SCALE
        k_slab = qkv_ref[0, :, GROUP_W + pair * LANES: GROUP_W + (pair + 1) * LANES]
        kmean = jnp.mean(k_slab.reshape(MOBA_NB, MOBA_BLOCK, LANES), axis=1)
        kmean2 = jnp.concatenate([jnp.where(lane_head == hh, kmean, 0.0) for hh in range(2)], axis=0)
        q_hi, q_lo = split(q_slab)
        k_hi, k_lo = split(kmean2)
        gate2 = _scores_t(jnp.concatenate([k_hi, k_lo, k_hi], axis=1),
                          jnp.concatenate([q_hi, q_hi, q_lo], axis=1))
        for hh in range(2):
            h = 2 * pair + hh
            gate = jnp.where(past, gate2[hh * MOBA_NB:(hh + 1) * MOBA_NB], NEG_INF)
            rank = jnp.zeros((MOBA_NB, S), jnp.int32)
            for jp in range(MOBA_NB):
                gj = gate[jp:jp + 1, :]
                beats = (gj > gate) | ((gj == gate) & (jp < blk_row))
                rank = rank + beats.astype(jnp.int32)
            keep = (past & (rank < MOBA_TOPK)) | (blk_row == q_blk)
            bias_t = jnp.where(keep, 0.0, NEG_INF)
            bias_t = jnp.concatenate([bias_t, jnp.zeros((LANES - MOBA_NB, S), F32)], axis=0)
            qa_ref[h] = jnp.concatenate([q_hi, bias_t.T.astype(BF16)], axis=1)
            ka_ref[h] = jnp.concatenate([jnp.where(lane_head == hh, k_slab, 0.0).astype(BF16), onehot], axis=1)

    def make_unit(i, h, outs):
        r0 = i * Q_BLOCK

        def score(j):
            s = _scores_t(ka_ref[h, j * KEY_CHUNK:(j + 1) * KEY_CHUNK, :], qa_ref[h, r0:r0 + Q_BLOCK, :])
            d = j - i * Q_CHUNKS
            return s + causal_ref[d * KEY_CHUNK:(d + 1) * KEY_CHUNK, :] if d >= 0 else s

        def done(out_t):
            outs.append(out_t)
            if len(outs) == N_HEADS:
                _store_heads(o_ref, r0, outs)

        return _AttnUnit((i + 1) * Q_CHUNKS, score, lambda j: vt_ref[h, :, j * KEY_CHUNK:(j + 1) * KEY_CHUNK],
                         s_ref.at[_score_slot(i * N_HEADS + h)], done)

    units = []
    for i in range(SEQ // Q_BLOCK):
        outs = []
        units += [make_unit(i, h, outs) for h in range(N_HEADS)]
    _run_interleaved(units, ATTN_STREAMS)


def _moba(qkv, causal_bias_t):
    B = qkv.shape[0]
    return pl.pallas_call(
        _moba_kernel,
        grid=(B,),
        in_specs=[
            pl.BlockSpec((1, SEQ, 3 * GROUP_W), lambda b: (b, 0, 0)),
            _resident((Q_BLOCK, Q_BLOCK)),
        ],
        out_specs=pl.BlockSpec((1, SEQ, GROUP_W), lambda b: (b, 0, 0)),
        out_shape=jax.ShapeDtypeStruct((B, SEQ, GROUP_W), BF16),
        scratch_shapes=[
            pltpu.VMEM((N_HEADS, SEQ, 2 * LANES), BF16),
            pltpu.VMEM((N_HEADS, SEQ, 2 * LANES), BF16),
            pltpu.VMEM((N_HEADS, VT_ROWS, SEQ), BF16),
            pltpu.VMEM((2 * ATTN_STREAMS, SEQ, Q_BLOCK), F32),
        ],
        compiler_params=_cparams("parallel"),
        name="moba",
    )(qkv, causal_bias_t)


def _dil_kernel(qkv_ref, bias_ref, o_ref, q_ref, k_ref, vt_ref, s_ref):
    for pair in range(2):
        _store_values_t(qkv_ref[0, :, 2 * GROUP_W + pair * LANES: 2 * GROUP_W + (pair + 1) * LANES], vt_ref, pair)
    lane_head = lax.broadcasted_iota(jnp.int32, (1, LANES), 1) // HEAD_DIM
    for pair in range(2):
        q_ref[pair] = (qkv_ref[0, :, pair * LANES:(pair + 1) * LANES] * Q_SCALE).astype(BF16)
        k_slab = qkv_ref[0, :, GROUP_W + pair * LANES: GROUP_W + (pair + 1) * LANES]
        for hh in range(2):
            k_ref[2 * pair + hh] = jnp.where(lane_head == hh, k_slab, 0.0).astype(BF16)

    def make_unit(i, h, outs):
        r0 = i * Q_BLOCK
        bias0 = SEQ - (r0 + Q_BLOCK)

        def score(j):
            keys = slice(j * KEY_CHUNK, (j + 1) * KEY_CHUNK)
            s = _scores_t(k_ref[h, keys, :], q_ref[h // 2, r0:r0 + Q_BLOCK, :])
            return s + bias_ref[bias0 + j * KEY_CHUNK: bias0 + (j + 1) * KEY_CHUNK, :]

        def done(out_t):
            outs.append(out_t)
            if len(outs) == N_HEADS:
                _store_heads(o_ref, r0, outs)

        return _AttnUnit((i + 1) * Q_CHUNKS, score, lambda j: vt_ref[h, :, j * KEY_CHUNK:(j + 1) * KEY_CHUNK],
                          s_ref.at[_score_slot(i * N_HEADS + h)], done)

    units = []
    for i in range(SEQ // Q_BLOCK):
        outs = []
        units += [make_unit(i, h, outs) for h in range(N_HEADS)]
    _run_interleaved(units, ATTN_STREAMS)


def _dil(qkv, dist_bias_t):
    B = qkv.shape[0]
    return pl.pallas_call(
        _dil_kernel,
        grid=(B,),
        in_specs=[
            pl.BlockSpec((1, SEQ, 3 * GROUP_W), lambda b: (b, 0, 0)),
            _resident((SEQ, Q_BLOCK)),
        ],
        out_specs=pl.BlockSpec((1, SEQ, GROUP_W), lambda b: (b, 0, 0)),
        out_shape=jax.ShapeDtypeStruct((B, SEQ, GROUP_W), BF16),
        scratch_shapes=[
            pltpu.VMEM((N_HEADS // 2, SEQ, LANES), BF16),
            pltpu.VMEM((N_HEADS, SEQ, LANES), BF16),
            pltpu.VMEM((N_HEADS, VT_ROWS, SEQ), BF16),
            pltpu.VMEM((2 * ATTN_STREAMS, SEQ, Q_BLOCK), F32),
        ],
        compiler_params=_cparams("parallel"),
        name="dilated",
    )(qkv, dist_bias_t)


def _causal_bias_t():
    r = np.arange(Q_BLOCK)
    return jnp.asarray(np.where(r[:, None] <= r[None, :], 0.0, NEG_INF), F32)


def _dilated_distance_bias_t():
    r = np.arange(Q_BLOCK)[None, :]
    x = np.arange(SEQ)[:, None]
    d = r - (x - (SEQ - Q_BLOCK))
    cnt = ((d >= 0) & (d <= 128)).astype(np.int64)
    cnt = cnt + ((d >= 0) & (d % 4 == 0) & (d <= 512))
    cnt = cnt + ((d >= 0) & (d % 16 == 0) & (d <= 2048))
    return jnp.asarray(np.where(cnt > 0, np.log2(np.maximum(cnt, 1)), NEG_INF), F32)


def _pool_blockdiag(pool_w):
    n = len(POOL_WINDOWS)
    eye = jnp.eye(n, dtype=pool_w.dtype)
    return jnp.einsum('gcd,gh->gchd', pool_w, eye).reshape(n * POOL_GROUP, n * POOL_GROUP)


def kernel(x, positions, ffn1_norm, ffn1_gate, ffn1_up, ffn1_down, mix_norm, w_in, pool_w, pool_scale,
           conv_w, conv_b, conv_ln_g, conv_ln_b, w_out, ffn2_norm, ffn2_gate, ffn2_up, ffn2_down, final_norm):
    B, S, D = x.shape
    assert (S, D) == (SEQ, D_MODEL)
    rope = [t.reshape(B * S, LANES) for t in _rope_tables(positions)]
    causal_t = _causal_bias_t()
    dist_bias_t = _dilated_distance_bias_t()
    row = lambda t: t.reshape(1, -1)
    rows = lambda t: t.reshape(DEPTH, 1, -1)
    conv_p = (conv_w, rows(conv_b), rows(conv_ln_g), rows(conv_ln_b))
    pre_mix = (ffn1_gate, ffn1_up, ffn1_down, w_in)
    post_mix = (ffn2_gate, ffn2_up, ffn2_down, w_out)
    wg, wu, wd, wi = (w[0].astype(BF16) for w in pre_mix)
    x2d = x.reshape(B * S, D)
    for l in range(DEPTH):
        x2d, u_pool, qkv_m, qkv_d, u_conv, wg, wu, wd, wo = _ffn_inproj(
            l, x2d, rows(ffn1_norm), wg, wu, wd, rows(mix_norm), wi, *rope, cast=(post_mix, l))
        seq = lambda t: t.reshape(B, S, -1)
        y_pool = _pool(seq(u_pool), _pool_blockdiag(pool_w[l]).astype(BF16), row(pool_scale[l]))
        y_moba = _moba(seq(qkv_m), causal_t)
        y_dil = _dil(seq(qkv_d), dist_bias_t)
        flat = lambda t: t.reshape(B * S, GROUP_W)
        last = l == DEPTH - 1
        x2d, *nxt = _mix_ffn(l, x2d, flat(y_pool), flat(y_moba), flat(y_dil), u_conv, *conv_p, wo,
                             rows(ffn2_norm), wg, wu, wd, row(final_norm), final_norm=last,
                             cast=None if last else (pre_mix, l + 1))
        if not last:
            wg, wu, wd, wi = nxt
    return x2d.reshape(B, S, D)
```

```python
import functools
import math

import numpy as np
import jax
import jax.numpy as jnp
from jax import lax
from jax.experimental import pallas as pl
from jax.experimental.pallas import tpu as pltpu

F32 = jnp.float32
BF16 = jnp.bfloat16

D_MODEL = 1024
SEQ = 2048
DEPTH = 2
HEAD_DIM = 64
N_HEADS = 4
GROUP_W = 256
POOL_WINDOWS = (2, 4, 8, 16)
POOL_GROUP = 64
MOBA_BLOCK = 256
MOBA_NB = SEQ // MOBA_BLOCK
MOBA_TOPK = 3
CONV_KERNEL = 31
ROPE_THETA = 500000.0
ROPE_DIMS = 16
D_FF = 2816
FF_CHUNK = 256
FFN_ROWS = 512
TOKEN_LAG = 1
D_IN = 2304
RMS_EPS = 1e-6
LN_EPS = 1e-5
NEG_INF = -1e30
ATTN_SCALE = HEAD_DIM ** -0.5
LOG2E = math.log2(math.e)
Q_SCALE = ATTN_SCALE * LOG2E
Q_BLOCK = 256
KEY_CHUNK = 256
Q_CHUNKS = Q_BLOCK // KEY_CHUNK
ATTN_STREAMS = 8
LANES = 128
SUBLANES = 8
BF16_ROWS = 16
VT_ROWS = HEAD_DIM + BF16_ROWS
CONV_PAD = 32
CONV_ROWS = 64

VMEM_LIMIT = 56 * 1024 * 1024


def _cparams(*sem):
    return pltpu.CompilerParams(dimension_semantics=sem, vmem_limit_bytes=VMEM_LIMIT)


def _rms(x, g):
    return x * lax.rsqrt(jnp.mean(x * x, axis=-1, keepdims=True) + RMS_EPS) * g


def _resident(shape):
    return pl.BlockSpec(shape, lambda *_: (0,) * len(shape), pipeline_mode=pl.Buffered(1))


def _layer(shape, l):
    return pl.BlockSpec((None,) + tuple(shape), lambda *_: (l,) + (0,) * len(shape), pipeline_mode=pl.Buffered(1))


def _zero_token(t):
    rows, cols = t.shape
    return jnp.minimum(jnp.abs(t.reshape(rows // SUBLANES, SUBLANES, cols)).max(axis=0), 0.0)


def _half_step_ffn(x, g_ref, wg_ref, wu_ref, wd_ref, hid_ref, side_work=()):
    n_up, n_down = D_FF // FF_CHUNK, D_MODEL // FF_CHUNK
    slots = [g for g in range(n_up + n_down - TOKEN_LAG) if g != n_up - 1]
    after = [[] for _ in range(n_up + n_down)]
    for k, piece in enumerate(side_work):
        after[slots[k * len(slots) // len(side_work)]].append(piece)

    def run_side_work(slot):
        token = None
        for piece in (after[slot] if slot < len(after) else ()):
            t = piece()
            token = t if token is None else token + t
        return token

    def gated(value, token):
        return value if token is None else jnp.concatenate([value[0:SUBLANES] + token, value[SUBLANES:]], axis=0)

    xn = _rms(x, g_ref[...]).astype(BF16)
    tokens = [None] * TOKEN_LAG
    for c in range(n_up):
        cols = slice(c * FF_CHUNK, (c + 1) * FF_CHUNK)
        gate = jnp.dot(xn, wg_ref[:, cols], preferred_element_type=F32)
        up = jnp.dot(xn, wu_ref[:, cols], preferred_element_type=F32)
        hid_ref[:, cols] = gated(jax.nn.silu(gate) * up, tokens.pop(0)).astype(BF16)
        tokens.append(run_side_work(c))
    outs = []
    for c in range(n_down):
        cols = slice(c * FF_CHUNK, (c + 1) * FF_CHUNK)
        y = x[:, cols] + 0.5 * jnp.dot(hid_ref[...], wd_ref[:, cols], preferred_element_type=F32)
        outs.append(gated(y, tokens.pop(0)))
        tokens.append(run_side_work(n_up + c))
    assert all(t is None for t in tokens)
    return jnp.concatenate(outs, axis=1)


def _conv_tile_pieces(u_ref, halo_fn, w_ref, b_ref, lg_ref, lb_ref, hs_ref, out_ref):
    tm = u_ref.shape[0]
    copy_rows = tm + CONV_PAD - SUBLANES
    first = CONV_PAD - (CONV_KERNEL - 1)

    def gate():
        h = u_ref[:, 0:GROUP_W] * jax.nn.sigmoid(u_ref[:, GROUP_W:])
        hs_ref[0, 0:CONV_PAD, :] = halo_fn()
        hs_ref[0, CONV_PAD:, :] = h
        return _zero_token(h)

    def copies(shifts):
        def piece():
            token = None
            for b in shifts:
                v = hs_ref[0, b:b + copy_rows, :]
                hs_ref[b, 0:copy_rows, :] = v
                token = _zero_token(v) if token is None else token + _zero_token(v)
            return token
        return piece

    def chunk(c):
        def piece():
            r0 = c * CONV_ROWS
            acc = None
            for j in range(CONV_KERNEL):
                a_, b_ = divmod(first + j, SUBLANES)
                term = w_ref[j:j + 1, :] * hs_ref[b_, r0 + SUBLANES * a_: r0 + SUBLANES * a_ + CONV_ROWS, :]
                acc = term if acc is None else acc + term
            y = acc + b_ref[...]
            mu = jnp.mean(y, axis=-1, keepdims=True)
            var = jnp.mean(jnp.square(y - mu), axis=-1, keepdims=True)
            z = (y - mu) * lax.rsqrt(var + LN_EPS) * lg_ref[...] + lb_ref[...]
            out = jax.nn.silu(z)
            out_ref[r0:r0 + CONV_ROWS, :] = out.astype(BF16)
            return _zero_token(out)
        return piece

    return [gate, copies((1, 2, 3, 4)), copies((5, 6, 7))] + [chunk(c) for c in range(tm // CONV_ROWS)]


def _rope(t, c, s_up, s_dn):
    half = ROPE_DIMS // 2
    return t * c + pltpu.roll(t, half, axis=1) * s_up + pltpu.roll(t, LANES - half, axis=1) * s_dn


def _cast_side_outputs(cast_in, cast_out):
    for src, dst in zip(cast_in, cast_out):
        dst[...] = src[...].astype(BF16)


def _ffn_inproj_kernel(x_ref, g_ref, wg_ref, wu_ref, wd_ref, gm_ref, wi_ref, c_ref, su_ref, sd_ref, *rest, n_cast):
    cast_in, (o_ref, pool_ref, moba_ref, dil_ref, conv_ref) = rest[:n_cast], rest[n_cast:n_cast + 5]
    cast_out, (hid_ref,) = rest[n_cast + 5:2 * n_cast + 5], rest[2 * n_cast + 5:]
    _cast_side_outputs(cast_in, cast_out)
    x = _half_step_ffn(x_ref[...], g_ref, wg_ref, wu_ref, wd_ref, hid_ref)
    o_ref[...] = x
    h = jnp.dot(_rms(x, gm_ref[...]).astype(BF16), wi_ref[...], preferred_element_type=F32)
    c, su, sd = c_ref[...], su_ref[...], sd_ref[...]
    pool_ref[...] = h[:, 0:GROUP_W]
    for out_ref, off in ((moba_ref, GROUP_W), (dil_ref, 4 * GROUP_W)):
        for part in range(6):
            piece = h[:, off + part * LANES: off + (part + 1) * LANES]
            if part < 4:
                piece = _rope(piece, c, su, sd)
            out_ref[:, part * LANES:(part + 1) * LANES] = piece
    conv_ref[...] = h[:, 7 * GROUP_W:]


def _mix_ffn_kernel(x_ref, yp_ref, ym_ref, yd_ref, u0_ref, un_ref, uh_ref, cw_ref, cb_ref, clg_ref, clb_ref,
                    wo_ref, g_ref, wg_ref, wu_ref, wd_ref, fin_ref, *rest, n_cast, final_norm):
    cast_in, o_ref, cast_out = rest[:n_cast], rest[n_cast], rest[n_cast + 1:2 * n_cast + 1]
    hid_ref, hs_ref, yc_ref = rest[2 * n_cast + 1:]
    _cast_side_outputs(cast_in, cast_out)
    i = pl.program_id(0)
    conv_refs = (cw_ref, cb_ref, clg_ref, clb_ref, hs_ref)

    @pl.when(i == 0)
    def _():
        zero_halo = lambda: jnp.zeros((CONV_PAD, GROUP_W), F32)
        for piece in _conv_tile_pieces(u0_ref, zero_halo, *conv_refs, yc_ref.at[0]):
            piece()

    def halo():
        h = uh_ref[:, 0:GROUP_W] * jax.nn.sigmoid(uh_ref[:, GROUP_W:])
        return jnp.where((i + 1) % (SEQ // FFN_ROWS) == 0, 0.0, h)

    next_conv = _conv_tile_pieces(un_ref, halo, *conv_refs, yc_ref.at[(i + 1) % 2])
    mix = jnp.concatenate([yp_ref[...], ym_ref[...], yd_ref[...], yc_ref[i % 2]], axis=1)
    x = x_ref[...] + jnp.dot(mix, wo_ref[...], preferred_element_type=F32)
    y = _half_step_ffn(x, g_ref, wg_ref, wu_ref, wd_ref, hid_ref, side_work=next_conv)
    if final_norm:
        y = _rms(y, fin_ref[...])
    o_ref[...] = y


def _ffn_weight_specs(l):
    return [_layer((1, D_MODEL), l), _resident((D_MODEL, D_FF)), _resident((D_MODEL, D_FF)),
            _resident((D_FF, D_MODEL))]


def _rows(width, tm):
    return pl.BlockSpec((tm, width), lambda i: (i, 0))


def _cast_specs(cast, n_steps):
    in_specs, out_specs, out_shapes, operands = [], [], [], []
    weights, layer = cast if cast is not None else ((), 0)
    for w in weights:
        _, rows, cols = w.shape
        rb = next(r for r in range(BF16_ROWS, rows + 1, BF16_ROWS) if rows % r == 0 and rows // r <= n_steps)
        last = rows // rb - 1
        in_specs.append(pl.BlockSpec((None, rb, cols), lambda i, last=last: (layer, jnp.minimum(i, last), 0)))
        out_specs.append(pl.BlockSpec((rb, cols), lambda i, last=last: (jnp.minimum(i, last), 0)))
        out_shapes.append(jax.ShapeDtypeStruct((rows, cols), BF16))
        operands.append(w)
    return in_specs, out_specs, out_shapes, operands


def _ffn_inproj(l, x2d, norm_g, wg, wu, wd, mix_g, w_in, rope_c, rope_su, rope_sd, *, cast=None, tm=FFN_ROWS):
    n_tok = x2d.shape[0]
    widths = (D_MODEL, GROUP_W, 3 * GROUP_W, 3 * GROUP_W, 2 * GROUP_W)
    c_in, c_out, c_shapes, c_ops = _cast_specs(cast, n_tok // tm)
    return pl.pallas_call(
        functools.partial(_ffn_inproj_kernel, n_cast=len(c_ops)),
        grid=(n_tok // tm,),
        in_specs=[_rows(D_MODEL, tm)] + _ffn_weight_specs(l)
                 + [_layer((1, D_MODEL), l), _resident((D_MODEL, D_IN))] + [_rows(LANES, tm)] * 3 + c_in,
        out_specs=[_rows(w, tm) for w in widths] + c_out,
        out_shape=[jax.ShapeDtypeStruct((n_tok, w), F32) for w in widths] + c_shapes,
        scratch_shapes=[pltpu.VMEM((tm, D_FF), BF16)],
        compiler_params=_cparams("arbitrary"),
        name="ffn_inproj",
    )(x2d, norm_g, wg, wu, wd, mix_g, w_in, rope_c, rope_su, rope_sd, *c_ops)


def _mix_ffn(l, x2d, yp, ym, yd, u_conv, conv_w, conv_b, conv_lg, conv_lb, w_out, norm_g, wg, wu, wd, fin_g,
             *, final_norm, cast=None, tm=FFN_ROWS):
    n_tok = x2d.shape[0]
    n_tiles = n_tok // tm
    halo_blocks = tm // CONV_PAD
    u_first = pl.BlockSpec((tm, 2 * GROUP_W), lambda i: (0, 0))
    u_next = pl.BlockSpec((tm, 2 * GROUP_W), lambda i: (jnp.minimum(i + 1, n_tiles - 1), 0))
    u_halo = pl.BlockSpec((CONV_PAD, 2 * GROUP_W),
                          lambda i: (jnp.minimum((i + 1) * halo_blocks - 1, n_tiles * halo_blocks - 1), 0))
    c_in, c_out, c_shapes, c_ops = _cast_specs(cast, n_tiles)
    return pl.pallas_call(
        functools.partial(_mix_ffn_kernel, n_cast=len(c_ops), final_norm=final_norm),
        grid=(n_tiles,),
        in_specs=[_rows(D_MODEL, tm)] + [_rows(GROUP_W, tm)] * 3 + [u_first, u_next, u_halo]
                 + [_layer((CONV_KERNEL, GROUP_W), l)] + [_layer((1, GROUP_W), l)] * 3
                 + [_resident((D_MODEL, D_MODEL))] + _ffn_weight_specs(l) + [_resident((1, D_MODEL))] + c_in,
        out_specs=[_rows(D_MODEL, tm)] + c_out,
        out_shape=[jax.ShapeDtypeStruct((n_tok, D_MODEL), F32)] + c_shapes,
        scratch_shapes=[pltpu.VMEM((tm, D_FF), BF16),
                        pltpu.VMEM((SUBLANES, tm + CONV_PAD, GROUP_W), F32),
                        pltpu.VMEM((2, tm, GROUP_W), BF16)],
        compiler_params=_cparams("arbitrary"),
        name="mix_ffn",
    )(x2d, yp, ym, yd, u_conv, u_conv, u_conv, conv_w, conv_b, conv_lg, conv_lb, w_out, norm_g, wg, wu, wd, fin_g,
      *c_ops)


def _rope_table_kernel(pos_ref, inv_ref, c_ref, su_ref, sd_ref):
    half = ROPE_DIMS // 2
    ang = pos_ref[0].astype(F32) * inv_ref[...]
    cos, sin = jnp.cos(ang), jnp.sin(ang)
    zeros = lambda n: jnp.zeros((n, SEQ), F32)
    rest = HEAD_DIM - ROPE_DIMS
    c = jnp.concatenate([cos, cos, jnp.ones((rest, SEQ), F32)] * 2, axis=0)
    s_up = jnp.concatenate([zeros(half), sin, zeros(rest)] * 2, axis=0)
    s_dn = jnp.concatenate([-sin, zeros(half), zeros(rest)] * 2, axis=0)
    c_ref[0] = c.T
    su_ref[0] = s_up.T
    sd_ref[0] = s_dn.T


def _rope_tables(positions):
    B = positions.shape[0]
    half = ROPE_DIMS // 2
    inv = ROPE_THETA ** (-jnp.arange(0, ROPE_DIMS, 2, dtype=F32) / ROPE_DIMS)
    table = jax.ShapeDtypeStruct((B, SEQ, LANES), F32)
    out_spec = pl.BlockSpec((1, SEQ, LANES), lambda b: (b, 0, 0))
    return pl.pallas_call(
        _rope_table_kernel,
        grid=(B,),
        in_specs=[pl.BlockSpec((1, 1, SEQ), lambda b: (b, 0, 0)), _resident((half, 1))],
        out_specs=[out_spec] * 3,
        out_shape=[table] * 3,
        compiler_params=_cparams("parallel"),
        name="rope_tables",
    )(positions.reshape(B, 1, SEQ), inv.reshape(half, 1))


def _pool_kernel(u_ref, w_ref, scale_ref, o_ref):
    u = u_ref[0]
    row = lax.broadcasted_iota(jnp.int32, u.shape, 0)
    lane = lax.broadcasted_iota(jnp.int32, u.shape, 1)

    def shifted(x, s):
        return jnp.where(row >= s, pltpu.roll(x, s, axis=0), 0.0)

    s2 = u + shifted(u, 1)
    s4 = s2 + shifted(s2, 2)
    s8 = s4 + shifted(s4, 4)
    s16 = s8 + shifted(s8, 8)
    grp = lane // POOL_GROUP
    wsum = jnp.where(grp == 0, s2, jnp.where(grp == 1, s4, jnp.where(grp == 2, s8, s16)))
    wnd = jnp.where(grp == 0, 2, jnp.where(grp == 1, 4, jnp.where(grp == 2, 8, 16)))
    cnt = jnp.minimum(row + 1, wnd).astype(F32)
    pooled = (wsum / cnt - u).astype(BF16)
    mixed = jnp.dot(pooled, w_ref[...], preferred_element_type=F32)
    o_ref[0] = (mixed * scale_ref[...]).astype(BF16)


def _pool(u, w_blockdiag, scale):
    B = u.shape[0]
    return pl.pallas_call(
        _pool_kernel,
        grid=(B,),
        in_specs=[
            pl.BlockSpec((1, SEQ, GROUP_W), lambda b: (b, 0, 0)),
            _resident((GROUP_W, GROUP_W)),
            _resident((1, GROUP_W)),
        ],
        out_specs=pl.BlockSpec((1, SEQ, GROUP_W), lambda b: (b, 0, 0)),
        out_shape=jax.ShapeDtypeStruct((B, SEQ, GROUP_W), BF16),
        compiler_params=_cparams("parallel"),
        name="pool",
    )(u, w_blockdiag, scale)


def _scores_t(k, q):
    return lax.dot_general(k, q, (((1,), (1,)), ((), ())), preferred_element_type=F32)


class _AttnUnit:
    def __init__(self, n_chunks, score_fn, vt_fn, done_fn):
        self.n_chunks, self.score_fn, self.vt_fn, self.done_fn = n_chunks, score_fn, vt_fn, done_fn
        self.m = None
        self.acc = None
        self.scores = {}

    def score(self, j):
        s = self.score_fn(j)
        self.scores[j] = s
        cm = s.max(axis=0, keepdims=True)
        self.m = cm if self.m is None else jnp.maximum(self.m, cm)

    def apply(self, j):
        p = jnp.exp2(self.scores.pop(j) - self.m).astype(BF16)
        pv = jnp.dot(self.vt_fn(j), p, preferred_element_type=F32)
        self.acc = pv if self.acc is None else self.acc + pv

    def finish(self):
        self.done_fn(self.acc[0:HEAD_DIM] / self.acc[HEAD_DIM:HEAD_DIM + 1])


def _pipeline(units):
    prev = None
    for unit in list(units) + [None]:
        n_score = unit.n_chunks if unit is not None else 0
        n_apply = prev.n_chunks if prev is not None else 0
        for j in range(max(n_score, n_apply)):
            if j < n_score:
                unit.score(j)
            if j < n_apply:
                prev.apply(j)
            yield
        if prev is not None:
            prev.finish()
        prev = unit


def _run_interleaved(units, streams=1):
    live = [_pipeline(units[s::streams]) for s in range(streams)]
    while live:
        live = [g for g in live if next(g, StopIteration) is not StopIteration]


def _store_values_t(v_slab_ref_slice, vt_ref, pair):
    vt = v_slab_ref_slice.T
    ones_blk = (lax.broadcasted_iota(jnp.int32, (BF16_ROWS, SEQ), 0) == 0).astype(BF16)
    for hh in range(2):
        h = 2 * pair + hh
        vt_ref[h, 0:HEAD_DIM, :] = vt[hh * HEAD_DIM:(hh + 1) * HEAD_DIM].astype(BF16)
        vt_ref[h, HEAD_DIM:VT_ROWS, :] = ones_blk


def _store_heads(o_ref, r0, outs_t):
    for pair in range(2):
        both = jnp.concatenate(outs_t[2 * pair:2 * pair + 2], axis=0)
        o_ref[0, r0:r0 + Q_BLOCK, pair * LANES:(pair + 1) * LANES] = both.T.astype(BF16)


def _moba_kernel(qkv_ref, causal_ref, o_ref, qa_ref, ka_ref, vt_ref):
    S = SEQ
    blk_row = lax.broadcasted_iota(jnp.int32, (MOBA_NB, S), 0)
    q_blk = lax.broadcasted_iota(jnp.int32, (MOBA_NB, S), 1) // MOBA_BLOCK
    past = blk_row < q_blk
    key_blk = lax.broadcasted_iota(jnp.int32, (S, LANES), 0) // MOBA_BLOCK
    onehot = (lax.broadcasted_iota(jnp.int32, (S, LANES), 1) == key_blk).astype(BF16)
    lane_head = lax.broadcasted_iota(jnp.int32, (1, LANES), 1) // HEAD_DIM

    def split(t):
        hi = t.astype(BF16)
        return hi, (t - hi.astype(F32)).astype(BF16)

    for pair in range(2):
        _store_values_t(qkv_ref[0, :, 2 * GROUP_W + pair * LANES: 2 * GROUP_W + (pair + 1) * LANES], vt_ref, pair)
        q_slab = qkv_ref[0, :, pair * LANES:(pair + 1) * LANES] * Q_SCALE
        k_slab = qkv_ref[0, :, GROUP_W + pair * LANES: GROUP_W + (pair + 1) * LANES]
        kmean = jnp.mean(k_slab.reshape(MOBA_NB, MOBA_BLOCK, LANES), axis=1)
        kmean2 = jnp.concatenate([jnp.where(lane_head == hh, kmean, 0.0) for hh in range(2)], axis=0)
        q_hi, q_lo = split(q_slab)
        k_hi, k_lo = split(kmean2)
        gate2 = _scores_t(jnp.concatenate([k_hi, k_lo, k_hi], axis=1),
                          jnp.concatenate([q_hi, q_hi, q_lo], axis=1))
        for hh in range(2):
            h = 2 * pair + hh
            gate = jnp.where(past, gate2[hh * MOBA_NB:(hh + 1) * MOBA_NB], NEG_INF)
            rank = jnp.zeros((MOBA_NB, S), jnp.int32)
            for jp in range(MOBA_NB):
                gj = gate[jp:jp + 1, :]
                beats = (gj > gate) | ((gj == gate) & (jp < blk_row))
                rank = rank + beats.astype(jnp.int32)
            keep = (past & (rank < MOBA_TOPK)) | (blk_row == q_blk)
            bias_t = jnp.where(keep, 0.0, NEG_INF)
            bias_t = jnp.concatenate([bias_t, jnp.zeros((LANES - MOBA_NB, S), F32)], axis=0)
            qa_ref[h] = jnp.concatenate([q_hi, bias_t.T.astype(BF16)], axis=1)
            ka_ref[h] = jnp.concatenate([jnp.where(lane_head == hh, k_slab, 0.0).astype(BF16), onehot], axis=1)

    def make_unit(i, h, outs):
        r0 = i * Q_BLOCK

        def score(j):
            s = _scores_t(ka_ref[h, j * KEY_CHUNK:(j + 1) * KEY_CHUNK, :], qa_ref[h, r0:r0 + Q_BLOCK, :])
            d = j - i * Q_CHUNKS
            return s + causal_ref[d * KEY_CHUNK:(d + 1) * KEY_CHUNK, :] if d >= 0 else s

        def done(out_t):
            outs[h] = out_t
            if len(outs) == N_HEADS:
                _store_heads(o_ref, r0, [outs[hd] for hd in range(N_HEADS)])

        return _AttnUnit((i + 1) * Q_CHUNKS, score, lambda j: vt_ref[h, :, j * KEY_CHUNK:(j + 1) * KEY_CHUNK], done)

    units = []
    for i in range(SEQ // Q_BLOCK):
        outs = {}
        units += [make_unit(i, h, outs) for h in range(N_HEADS)]
    _run_interleaved(units, ATTN_STREAMS)


def _moba(qkv, causal_bias_t):
    B = qkv.shape[0]
    return pl.pallas_call(
        _moba_kernel,
        grid=(B,),
        in_specs=[
            pl.BlockSpec((1, SEQ, 3 * GROUP_W), lambda b: (b, 0, 0)),
            _resident((Q_BLOCK, Q_BLOCK)),
        ],
        out_specs=pl.BlockSpec((1, SEQ, GROUP_W), lambda b: (b, 0, 0)),
        out_shape=jax.ShapeDtypeStruct((B, SEQ, GROUP_W), BF16),
        scratch_shapes=[
            pltpu.VMEM((N_HEADS, SEQ, 2 * LANES), BF16),
            pltpu.VMEM((N_HEADS, SEQ, 2 * LANES), BF16),
            pltpu.VMEM((N_HEADS, VT_ROWS, SEQ), BF16),
        ],
        compiler_params=_cparams("parallel"),
        name="moba",
    )(qkv, causal_bias_t)


def _dil_kernel(qkv_ref, bias_ref, o_ref, q_ref, k_ref, vt_ref):
    for pair in range(2):
        _store_values_t(qkv_ref[0, :, 2 * GROUP_W + pair * LANES: 2 * GROUP_W + (pair + 1) * LANES], vt_ref, pair)
    lane_head = lax.broadcasted_iota(jnp.int32, (1, LANES), 1) // HEAD_DIM
    for pair in range(2):
        q_ref[pair] = (qkv_ref[0, :, pair * LANES:(pair + 1) * LANES] * Q_SCALE).astype(BF16)
        k_slab = qkv_ref[0, :, GROUP_W + pair * LANES: GROUP_W + (pair + 1) * LANES]
        for hh in range(2):
            k_ref[2 * pair + hh] = jnp.where(lane_head == hh, k_slab, 0.0).astype(BF16)

    def make_unit(i, h, outs):
        r0 = i * Q_BLOCK
        bias0 = SEQ - (r0 + Q_BLOCK)

        def score(j):
            keys = slice(j * KEY_CHUNK, (j + 1) * KEY_CHUNK)
            s = _scores_t(k_ref[h, keys, :], q_ref[h // 2, r0:r0 + Q_BLOCK, :])
            return s + bias_ref[bias0 + j * KEY_CHUNK: bias0 + (j + 1) * KEY_CHUNK, :]

        def done(out_t):
            outs[h] = out_t
            if len(outs) == N_HEADS:
                _store_heads(o_ref, r0, [outs[hd] for hd in range(N_HEADS)])

        return _AttnUnit((i + 1) * Q_CHUNKS, score, lambda j: vt_ref[h, :, j * KEY_CHUNK:(j + 1) * KEY_CHUNK], done)

    units = []
    for i in range(SEQ // Q_BLOCK):
        outs = {}
        units += [make_unit(i, h, outs) for h in range(N_HEADS)]
    _run_interleaved(units, ATTN_STREAMS)


def _dil(qkv, dist_bias_t):
    B = qkv.shape[0]
    return pl.pallas_call(
        _dil_kernel,
        grid=(B,),
        in_specs=[
            pl.BlockSpec((1, SEQ, 3 * GROUP_W), lambda b: (b, 0, 0)),
            _resident((SEQ, Q_BLOCK)),
        ],
        out_specs=pl.BlockSpec((1, SEQ, GROUP_W), lambda b: (b, 0, 0)),
        out_shape=jax.ShapeDtypeStruct((B, SEQ, GROUP_W), BF16),
        scratch_shapes=[
            pltpu.VMEM((N_HEADS // 2, SEQ, LANES), BF16),
            pltpu.VMEM((N_HEADS, SEQ, LANES), BF16),
            pltpu.VMEM((N_HEADS, VT_ROWS, SEQ), BF16),
        ],
        compiler_params=_cparams("parallel"),
        name="dilated",
    )(qkv, dist_bias_t)


def _causal_bias_t():
    r = np.arange(Q_BLOCK)
    return jnp.asarray(np.where(r[:, None] <= r[None, :], 0.0, NEG_INF), F32)


def _dilated_distance_bias_t():
    r = np.arange(Q_BLOCK)[None, :]
    x = np.arange(SEQ)[:, None]
    d = r - (x - (SEQ - Q_BLOCK))
    cnt = ((d >= 0) & (d <= 128)).astype(np.int64)
    cnt = cnt + ((d >= 0) & (d % 4 == 0) & (d <= 512))
    cnt = cnt + ((d >= 0) & (d % 16 == 0) & (d <= 2048))
    return jnp.asarray(np.where(cnt > 0, np.log2(np.maximum(cnt, 1)), NEG_INF), F32)


def _pool_blockdiag(pool_w):
    n = len(POOL_WINDOWS)
    eye = jnp.eye(n, dtype=pool_w.dtype)
    return jnp.einsum('gcd,gh->gchd', pool_w, eye).reshape(n * POOL_GROUP, n * POOL_GROUP)


def kernel(x, positions, ffn1_norm, ffn1_gate, ffn1_up, ffn1_down, mix_norm, w_in, pool_w, pool_scale,
           conv_w, conv_b, conv_ln_g, conv_ln_b, w_out, ffn2_norm, ffn2_gate, ffn2_up, ffn2_down, final_norm):
    B, S, D = x.shape
    assert (S, D) == (SEQ, D_MODEL)
    rope = [t.reshape(B * S, LANES) for t in _rope_tables(positions)]
    causal_t = _causal_bias_t()
    dist_bias_t = _dilated_distance_bias_t()
    row = lambda t: t.reshape(1, -1)
    rows = lambda t: t.reshape(DEPTH, 1, -1)
    conv_p = (conv_w, rows(conv_b), rows(conv_ln_g), rows(conv_ln_b))
    pre_mix = (ffn1_gate, ffn1_up, ffn1_down, w_in)
    post_mix = (ffn2_gate, ffn2_up, ffn2_down, w_out)
    wg, wu, wd, wi = (w[0].astype(BF16) for w in pre_mix)
    x2d = x.reshape(B * S, D)
    for l in range(DEPTH):
        x2d, u_pool, qkv_m, qkv_d, u_conv, wg, wu, wd, wo = _ffn_inproj(
            l, x2d, rows(ffn1_norm), wg, wu, wd, rows(mix_norm), wi, *rope, cast=(post_mix, l))
        seq = lambda t: t.reshape(B, S, -1)
        y_pool = _pool(seq(u_pool), _pool_blockdiag(pool_w[l]).astype(BF16), row(pool_scale[l]))
        y_moba = _moba(seq(qkv_m), causal_t)
        y_dil = _dil(seq(qkv_d), dist_bias_t)
        flat = lambda t: t.reshape(B * S, GROUP_W)
        last = l == DEPTH - 1
        x2d, *nxt = _mix_ffn(l, x2d, flat(y_pool), flat(y_moba), flat(y_dil), u_conv, *conv_p, wo,
                             rows(ffn2_norm), wg, wu, wd, row(final_norm), final_norm=last,
                             cast=None if last else (pre_mix, l + 1))
        if not last:
            wg, wu, wd, wi = nxt
    return x2d.reshape(B, S, D)
```

```python
import functools
import math

import numpy as np
import jax
import jax.numpy as jnp
from jax import lax
from jax.experimental import pallas as pl
from jax.experimental.pallas import tpu as pltpu

F32 = jnp.float32
BF16 = jnp.bfloat16

D_MODEL = 1024
SEQ = 2048
DEPTH = 2
HEAD_DIM = 64
N_HEADS = 4
GROUP_W = 256
POOL_WINDOWS = (2, 4, 8, 16)
POOL_GROUP = 64
MOBA_BLOCK = 256
MOBA_NB = SEQ // MOBA_BLOCK
MOBA_TOPK = 3
CONV_KERNEL = 31
ROPE_THETA = 500000.0
ROPE_DIMS = 16
D_FF = 2816
FF_CHUNK = 256
FFN_ROWS = 512
TOKEN_LAG = 1
D_IN = 2304
RMS_EPS = 1e-6
LN_EPS = 1e-5
NEG_INF = -1e30
ATTN_SCALE = HEAD_DIM ** -0.5
LOG2E = math.log2(math.e)
Q_SCALE = ATTN_SCALE * LOG2E
Q_BLOCK = 256
KEY_CHUNK = 256
Q_CHUNKS = Q_BLOCK // KEY_CHUNK
MOBA_STREAMS = 8
DIL_STREAMS = 4
LANES = 128
SUBLANES = 8
BF16_ROWS = 16
VT_ROWS = HEAD_DIM + BF16_ROWS
CONV_PAD = 32
CONV_ROWS = 64

VMEM_LIMIT = 56 * 1024 * 1024


def _cparams(*sem):
    return pltpu.CompilerParams(dimension_semantics=sem, vmem_limit_bytes=VMEM_LIMIT)


def _rms(x, g):
    return x * lax.rsqrt(jnp.mean(x * x, axis=-1, keepdims=True) + RMS_EPS) * g


def _resident(shape):
    return pl.BlockSpec(shape, lambda *_: (0,) * len(shape), pipeline_mode=pl.Buffered(1))


def _layer(shape, l):
    return pl.BlockSpec((None,) + tuple(shape), lambda *_: (l,) + (0,) * len(shape), pipeline_mode=pl.Buffered(1))


def _zero_token(t):
    rows, cols = t.shape
    return jnp.minimum(jnp.abs(t.reshape(rows // SUBLANES, SUBLANES, cols)).max(axis=0), 0.0)


def _half_step_ffn(x, g_ref, wg_ref, wu_ref, wd_ref, hid_ref, side_work=()):
    n_up, n_down = D_FF // FF_CHUNK, D_MODEL // FF_CHUNK
    slots = [g for g in range(n_up + n_down - TOKEN_LAG) if g != n_up - 1]
    after = [[] for _ in range(n_up + n_down)]
    for k, piece in enumerate(side_work):
        after[slots[k * len(slots) // len(side_work)]].append(piece)

    def run_side_work(slot):
        token = None
        for piece in (after[slot] if slot < len(after) else ()):
            t = piece()
            token = t if token is None else token + t
        return token

    def gated(value, token):
        return value if token is None else jnp.concatenate([value[0:SUBLANES] + token, value[SUBLANES:]], axis=0)

    xn = _rms(x, g_ref[...]).astype(BF16)
    tokens = [None] * TOKEN_LAG
    for c in range(n_up):
        cols = slice(c * FF_CHUNK, (c + 1) * FF_CHUNK)
        gate = jnp.dot(xn, wg_ref[:, cols], preferred_element_type=F32)
        up = jnp.dot(xn, wu_ref[:, cols], preferred_element_type=F32)
        hid_ref[:, cols] = gated(jax.nn.silu(gate) * up, tokens.pop(0)).astype(BF16)
        tokens.append(run_side_work(c))
    outs = []
    for c in range(n_down):
        cols = slice(c * FF_CHUNK, (c + 1) * FF_CHUNK)
        y = x[:, cols] + 0.5 * jnp.dot(hid_ref[...], wd_ref[:, cols], preferred_element_type=F32)
        outs.append(gated(y, tokens.pop(0)))
        tokens.append(run_side_work(n_up + c))
    assert all(t is None for t in tokens)
    return jnp.concatenate(outs, axis=1)


def _conv_tile_pieces(u_ref, halo_fn, w_ref, b_ref, lg_ref, lb_ref, hs_ref, out_ref):
    tm = u_ref.shape[0]
    copy_rows = tm + CONV_PAD - SUBLANES
    first = CONV_PAD - (CONV_KERNEL - 1)

    def gate():
        h = u_ref[:, 0:GROUP_W] * jax.nn.sigmoid(u_ref[:, GROUP_W:])
        hs_ref[0, 0:CONV_PAD, :] = halo_fn()
        hs_ref[0, CONV_PAD:, :] = h
        return _zero_token(h)

    def copies(shifts):
        def piece():
            token = None
            for b in shifts:
                v = hs_ref[0, b:b + copy_rows, :]
                hs_ref[b, 0:copy_rows, :] = v
                token = _zero_token(v) if token is None else token + _zero_token(v)
            return token
        return piece

    def chunk(c):
        def piece():
            r0 = c * CONV_ROWS
            acc = None
            for j in range(CONV_KERNEL):
                a_, b_ = divmod(first + j, SUBLANES)
                term = w_ref[j:j + 1, :] * hs_ref[b_, r0 + SUBLANES * a_: r0 + SUBLANES * a_ + CONV_ROWS, :]
                acc = term if acc is None else acc + term
            y = acc + b_ref[...]
            mu = jnp.mean(y, axis=-1, keepdims=True)
            var = jnp.mean(jnp.square(y - mu), axis=-1, keepdims=True)
            z = (y - mu) * lax.rsqrt(var + LN_EPS) * lg_ref[...] + lb_ref[...]
            out = jax.nn.silu(z)
            out_ref[r0:r0 + CONV_ROWS, :] = out.astype(BF16)
            return _zero_token(out)
        return piece

    return [gate, copies((1, 2, 3, 4)), copies((5, 6, 7))] + [chunk(c) for c in range(tm // CONV_ROWS)]


def _rope(t, c, s_up, s_dn):
    half = ROPE_DIMS // 2
    return t * c + pltpu.roll(t, half, axis=1) * s_up + pltpu.roll(t, LANES - half, axis=1) * s_dn


def _cast_side_outputs(cast_in, cast_out):
    for src, dst in zip(cast_in, cast_out):
        dst[...] = src[...].astype(BF16)


def _ffn_inproj_kernel(x_ref, g_ref, wg_ref, wu_ref, wd_ref, gm_ref, wi_ref, c_ref, su_ref, sd_ref, *rest, n_cast):
    cast_in, (o_ref, pool_ref, moba_ref, dil_ref, conv_ref) = rest[:n_cast], rest[n_cast:n_cast + 5]
    cast_out, (hid_ref,) = rest[n_cast + 5:2 * n_cast + 5], rest[2 * n_cast + 5:]
    _cast_side_outputs(cast_in, cast_out)
    x = _half_step_ffn(x_ref[...], g_ref, wg_ref, wu_ref, wd_ref, hid_ref)
    o_ref[...] = x
    h = jnp.dot(_rms(x, gm_ref[...]).astype(BF16), wi_ref[...], preferred_element_type=F32)
    c, su, sd = c_ref[...], su_ref[...], sd_ref[...]
    pool_ref[...] = h[:, 0:GROUP_W]
    for out_ref, off in ((moba_ref, GROUP_W), (dil_ref, 4 * GROUP_W)):
        for part in range(6):
            piece = h[:, off + part * LANES: off + (part + 1) * LANES]
            if part < 4:
                piece = _rope(piece, c, su, sd)
            out_ref[:, part * LANES:(part + 1) * LANES] = piece
    conv_ref[...] = h[:, 7 * GROUP_W:]


def _mix_ffn_kernel(x_ref, yp_ref, ym_ref, yd_ref, u0_ref, un_ref, uh_ref, cw_ref, cb_ref, clg_ref, clb_ref,
                    wo_ref, g_ref, wg_ref, wu_ref, wd_ref, fin_ref, *rest, n_cast, final_norm):
    cast_in, o_ref, cast_out = rest[:n_cast], rest[n_cast], rest[n_cast + 1:2 * n_cast + 1]
    hid_ref, hs_ref, yc_ref = rest[2 * n_cast + 1:]
    _cast_side_outputs(cast_in, cast_out)
    i = pl.program_id(0)
    conv_refs = (cw_ref, cb_ref, clg_ref, clb_ref, hs_ref)

    @pl.when(i == 0)
    def _():
        zero_halo = lambda: jnp.zeros((CONV_PAD, GROUP_W), F32)
        for piece in _conv_tile_pieces(u0_ref, zero_halo, *conv_refs, yc_ref.at[0]):
            piece()

    def halo():
        h = uh_ref[:, 0:GROUP_W] * jax.nn.sigmoid(uh_ref[:, GROUP_W:])
        return jnp.where((i + 1) % (SEQ // FFN_ROWS) == 0, 0.0, h)

    next_conv = _conv_tile_pieces(un_ref, halo, *conv_refs, yc_ref.at[(i + 1) % 2])
    mix = jnp.concatenate([yp_ref[...], ym_ref[...], yd_ref[...], yc_ref[i % 2]], axis=1)
    x = x_ref[...] + jnp.dot(mix, wo_ref[...], preferred_element_type=F32)
    y = _half_step_ffn(x, g_ref, wg_ref, wu_ref, wd_ref, hid_ref, side_work=next_conv)
    if final_norm:
        y = _rms(y, fin_ref[...])
    o_ref[...] = y


def _ffn_weight_specs(l):
    return [_layer((1, D_MODEL), l), _resident((D_MODEL, D_FF)), _resident((D_MODEL, D_FF)),
            _resident((D_FF, D_MODEL))]


def _rows(width, tm):
    return pl.BlockSpec((tm, width), lambda i: (i, 0))


def _cast_specs(cast, n_steps):
    in_specs, out_specs, out_shapes, operands = [], [], [], []
    weights, layer = cast if cast is not None else ((), 0)
    for w in weights:
        _, rows, cols = w.shape
        rb = next(r for r in range(BF16_ROWS, rows + 1, BF16_ROWS) if rows % r == 0 and rows // r <= n_steps)
        last = rows // rb - 1
        in_specs.append(pl.BlockSpec((None, rb, cols), lambda i, last=last: (layer, jnp.minimum(i, last), 0)))
        out_specs.append(pl.BlockSpec((rb, cols), lambda i, last=last: (jnp.minimum(i, last), 0)))
        out_shapes.append(jax.ShapeDtypeStruct((rows, cols), BF16))
        operands.append(w)
    return in_specs, out_specs, out_shapes, operands


def _ffn_inproj(l, x2d, norm_g, wg, wu, wd, mix_g, w_in, rope_c, rope_su, rope_sd, *, cast=None, tm=FFN_ROWS):
    n_tok = x2d.shape[0]
    widths = (D_MODEL, GROUP_W, 3 * GROUP_W, 3 * GROUP_W, 2 * GROUP_W)
    c_in, c_out, c_shapes, c_ops = _cast_specs(cast, n_tok // tm)
    return pl.pallas_call(
        functools.partial(_ffn_inproj_kernel, n_cast=len(c_ops)),
        grid=(n_tok // tm,),
        in_specs=[_rows(D_MODEL, tm)] + _ffn_weight_specs(l)
                 + [_layer((1, D_MODEL), l), _resident((D_MODEL, D_IN))] + [_rows(LANES, tm)] * 3 + c_in,
        out_specs=[_rows(w, tm) for w in widths] + c_out,
        out_shape=[jax.ShapeDtypeStruct((n_tok, w), F32) for w in widths] + c_shapes,
        scratch_shapes=[pltpu.VMEM((tm, D_FF), BF16)],
        compiler_params=_cparams("arbitrary"),
        name="ffn_inproj",
    )(x2d, norm_g, wg, wu, wd, mix_g, w_in, rope_c, rope_su, rope_sd, *c_ops)


def _mix_ffn(l, x2d, yp, ym, yd, u_conv, conv_w, conv_b, conv_lg, conv_lb, w_out, norm_g, wg, wu, wd, fin_g,
             *, final_norm, cast=None, tm=FFN_ROWS):
    n_tok = x2d.shape[0]
    n_tiles = n_tok // tm
    halo_blocks = tm // CONV_PAD
    u_first = pl.BlockSpec((tm, 2 * GROUP_W), lambda i: (0, 0))
    u_next = pl.BlockSpec((tm, 2 * GROUP_W), lambda i: (jnp.minimum(i + 1, n_tiles - 1), 0))
    u_halo = pl.BlockSpec((CONV_PAD, 2 * GROUP_W),
                          lambda i: (jnp.minimum((i + 1) * halo_blocks - 1, n_tiles * halo_blocks - 1), 0))
    c_in, c_out, c_shapes, c_ops = _cast_specs(cast, n_tiles)
    return pl.pallas_call(
        functools.partial(_mix_ffn_kernel, n_cast=len(c_ops), final_norm=final_norm),
        grid=(n_tiles,),
        in_specs=[_rows(D_MODEL, tm)] + [_rows(GROUP_W, tm)] * 3 + [u_first, u_next, u_halo]
                 + [_layer((CONV_KERNEL, GROUP_W), l)] + [_layer((1, GROUP_W), l)] * 3
                 + [_resident((D_MODEL, D_MODEL))] + _ffn_weight_specs(l) + [_resident((1, D_MODEL))] + c_in,
        out_specs=[_rows(D_MODEL, tm)] + c_out,
        out_shape=[jax.ShapeDtypeStruct((n_tok, D_MODEL), F32)] + c_shapes,
        scratch_shapes=[pltpu.VMEM((tm, D_FF), BF16),
                        pltpu.VMEM((SUBLANES, tm + CONV_PAD, GROUP_W), F32),
                        pltpu.VMEM((2, tm, GROUP_W), BF16)],
        compiler_params=_cparams("arbitrary"),
        name="mix_ffn",
    )(x2d, yp, ym, yd, u_conv, u_conv, u_conv, conv_w, conv_b, conv_lg, conv_lb, w_out, norm_g, wg, wu, wd, fin_g,
      *c_ops)


def _rope_table_kernel(pos_ref, inv_ref, c_ref, su_ref, sd_ref):
    half = ROPE_DIMS // 2
    ang = pos_ref[0].astype(F32) * inv_ref[...]
    cos, sin = jnp.cos(ang), jnp.sin(ang)
    zeros = lambda n: jnp.zeros((n, SEQ), F32)
    rest = HEAD_DIM - ROPE_DIMS
    c = jnp.concatenate([cos, cos, jnp.ones((rest, SEQ), F32)] * 2, axis=0)
    s_up = jnp.concatenate([zeros(half), sin, zeros(rest)] * 2, axis=0)
    s_dn = jnp.concatenate([-sin, zeros(half), zeros(rest)] * 2, axis=0)
    c_ref[0] = c.T
    su_ref[0] = s_up.T
    sd_ref[0] = s_dn.T


def _rope_tables(positions):
    B = positions.shape[0]
    half = ROPE_DIMS // 2
    inv = ROPE_THETA ** (-jnp.arange(0, ROPE_DIMS, 2, dtype=F32) / ROPE_DIMS)
    table = jax.ShapeDtypeStruct((B, SEQ, LANES), F32)
    out_spec = pl.BlockSpec((1, SEQ, LANES), lambda b: (b, 0, 0))
    return pl.pallas_call(
        _rope_table_kernel,
        grid=(B,),
        in_specs=[pl.BlockSpec((1, 1, SEQ), lambda b: (b, 0, 0)), _resident((half, 1))],
        out_specs=[out_spec] * 3,
        out_shape=[table] * 3,
        compiler_params=_cparams("parallel"),
        name="rope_tables",
    )(positions.reshape(B, 1, SEQ), inv.reshape(half, 1))


def _pool_kernel(u_ref, cnt_ref, w_ref, scale_ref, o_ref):
    row = lax.broadcasted_iota(jnp.int32, (SEQ, LANES), 0)
    low_group = lax.broadcasted_iota(jnp.int32, (SEQ, LANES), 1) < POOL_GROUP

    def shifted(x, s):
        return jnp.where(row >= s, pltpu.roll(x, s, axis=0), 0.0)

    pooled = []
    for c in range(GROUP_W // LANES):
        lanes = slice(c * LANES, (c + 1) * LANES)
        u = u_ref[0, :, lanes]
        lo, hi = POOL_WINDOWS[2 * c], POOL_WINDOWS[2 * c + 1]
        sums, w = {1: u}, 1
        while w < hi:
            sums[2 * w] = sums[w] + shifted(sums[w], w)
            w *= 2
        wsum = jnp.where(low_group, sums[lo], sums[hi])
        pooled.append((wsum / cnt_ref[:, lanes] - u).astype(BF16))
    mixed = jnp.dot(jnp.concatenate(pooled, axis=1), w_ref[...], preferred_element_type=F32)
    o_ref[0] = (mixed * scale_ref[...]).astype(BF16)


def _pool_counts():
    t = np.arange(SEQ)[:, None] + 1
    wnd = np.repeat(np.asarray(POOL_WINDOWS), POOL_GROUP)[None, :]
    return jnp.asarray(np.minimum(t, wnd), F32)


def _pool(u, counts, w_blockdiag, scale):
    B = u.shape[0]
    return pl.pallas_call(
        _pool_kernel,
        grid=(B,),
        in_specs=[
            pl.BlockSpec((1, SEQ, GROUP_W), lambda b: (b, 0, 0)),
            _resident((SEQ, GROUP_W)),
            _resident((GROUP_W, GROUP_W)),
            _resident((1, GROUP_W)),
        ],
        out_specs=pl.BlockSpec((1, SEQ, GROUP_W), lambda b: (b, 0, 0)),
        out_shape=jax.ShapeDtypeStruct((B, SEQ, GROUP_W), BF16),
        compiler_params=_cparams("parallel"),
        name="pool",
    )(u, counts, w_blockdiag, scale)


def _scores_t(k, q):
    return lax.dot_general(k, q, (((1,), (1,)), ((), ())), preferred_element_type=F32)


class _AttnUnit:
    def __init__(self, n_chunks, score_fn, vt_fn, done_fn):
        self.n_chunks, self.score_fn, self.vt_fn, self.done_fn = n_chunks, score_fn, vt_fn, done_fn
        self.m = None
        self.acc = None
        self.scores = {}

    def score(self, j):
        s = self.score_fn(j)
        self.scores[j] = s
        cm = s.max(axis=0, keepdims=True)
        self.m = cm if self.m is None else jnp.maximum(self.m, cm)

    def apply(self, j):
        p = jnp.exp2(self.scores.pop(j) - self.m).astype(BF16)
        pv = jnp.dot(self.vt_fn(j), p, preferred_element_type=F32)
        self.acc = pv if self.acc is None else self.acc + pv

    def finish(self):
        self.done_fn(self.acc[0:HEAD_DIM] / self.acc[HEAD_DIM:HEAD_DIM + 1])


def _pipeline(units):
    prev = None
    for unit in list(units) + [None]:
        n_score = unit.n_chunks if unit is not None else 0
        n_apply = prev.n_chunks if prev is not None else 0
        for j in range(max(n_score, n_apply)):
            if j < n_score:
                unit.score(j)
            if j < n_apply:
                prev.apply(j)
            yield
        if prev is not None:
            prev.finish()
        prev = unit


def _run_interleaved(units, streams=1):
    live = [_pipeline(units[s::streams]) for s in range(streams)]
    while live:
        live = [g for g in live if next(g, StopIteration) is not StopIteration]


def _store_values_t(v_slab_ref_slice, vt_ref, pair):
    vt = v_slab_ref_slice.T
    ones_blk = (lax.broadcasted_iota(jnp.int32, (BF16_ROWS, SEQ), 0) == 0).astype(BF16)
    for hh in range(2):
        h = 2 * pair + hh
        vt_ref[h, 0:HEAD_DIM, :] = vt[hh * HEAD_DIM:(hh + 1) * HEAD_DIM].astype(BF16)
        vt_ref[h, HEAD_DIM:VT_ROWS, :] = ones_blk


def _store_heads(o_ref, r0, outs_t):
    for pair in range(2):
        both = jnp.concatenate(outs_t[2 * pair:2 * pair + 2], axis=0)
        o_ref[0, r0:r0 + Q_BLOCK, pair * LANES:(pair + 1) * LANES] = both.T.astype(BF16)


def _moba_kernel(qkv_ref, causal_ref, o_ref, qa_ref, ka_ref, vt_ref):
    S = SEQ
    blk_row = lax.broadcasted_iota(jnp.int32, (MOBA_NB, S), 0)
    q_blk = lax.broadcasted_iota(jnp.int32, (MOBA_NB, S), 1) // MOBA_BLOCK
    past = blk_row < q_blk
    key_blk = lax.broadcasted_iota(jnp.int32, (S, LANES), 0) // MOBA_BLOCK
    onehot = (lax.broadcasted_iota(jnp.int32, (S, LANES), 1) == key_blk).astype(BF16)
    lane_head = lax.broadcasted_iota(jnp.int32, (1, LANES), 1) // HEAD_DIM

    def split(t):
        hi = t.astype(BF16)
        return hi, (t - hi.astype(F32)).astype(BF16)

    for pair in range(2):
        _store_values_t(qkv_ref[0, :, 2 * GROUP_W + pair * LANES: 2 * GROUP_W + (pair + 1) * LANES], vt_ref, pair)
        q_slab = qkv_ref[0, :, pair * LANES:(pair + 1) * LANES] * Q_SCALE
        k_slab = qkv_ref[0, :, GROUP_W + pair * LANES: GROUP_W + (pair + 1) * LANES]
        kmean = jnp.mean(k_slab.reshape(MOBA_NB, MOBA_BLOCK, LANES), axis=1)
        kmean2 = jnp.concatenate([jnp.where(lane_head == hh, kmean, 0.0) for hh in range(2)], axis=0)
        q_hi, q_lo = split(q_slab)
        k_hi, k_lo = split(kmean2)
        gate2 = _scores_t(jnp.concatenate([k_hi, k_lo, k_hi], axis=1),
                          jnp.concatenate([q_hi, q_hi, q_lo], axis=1))
        for hh in range(2):
            h = 2 * pair + hh
            gate = jnp.where(past, gate2[hh * MOBA_NB:(hh + 1) * MOBA_NB], NEG_INF)
            rank = jnp.zeros((MOBA_NB, S), jnp.int32)
            for jp in range(MOBA_NB):
                gj = gate[jp:jp + 1, :]
                beats = (gj > gate) | ((gj == gate) & (jp < blk_row))
                rank = rank + beats.astype(jnp.int32)
            keep = (past & (rank < MOBA_TOPK)) | (blk_row == q_blk)
            bias_t = jnp.where(keep, 0.0, NEG_INF)
            bias_t = jnp.concatenate([bias_t, jnp.zeros((LANES - MOBA_NB, S), F32)], axis=0)
            qa_ref[h] = jnp.concatenate([q_hi, bias_t.T.astype(BF16)], axis=1)
            ka_ref[h] = jnp.concatenate([jnp.where(lane_head == hh, k_slab, 0.0).astype(BF16), onehot], axis=1)

    def make_unit(i, h, outs):
        r0 = i * Q_BLOCK

        def score(j):
            s = _scores_t(ka_ref[h, j * KEY_CHUNK:(j + 1) * KEY_CHUNK, :], qa_ref[h, r0:r0 + Q_BLOCK, :])
            d = j - i * Q_CHUNKS
            return s + causal_ref[d * KEY_CHUNK:(d + 1) * KEY_CHUNK, :] if d >= 0 else s

        def done(out_t):
            outs[h] = out_t
            if len(outs) == N_HEADS:
                _store_heads(o_ref, r0, [outs[hd] for hd in range(N_HEADS)])

        return _AttnUnit((i + 1) * Q_CHUNKS, score, lambda j: vt_ref[h, :, j * KEY_CHUNK:(j + 1) * KEY_CHUNK], done)

    units = []
    for i in range(SEQ // Q_BLOCK):
        outs = {}
        units += [make_unit(i, h, outs) for h in range(N_HEADS)]
    _run_interleaved(units, MOBA_STREAMS)


def _moba(qkv, causal_bias_t):
    B = qkv.shape[0]
    return pl.pallas_call(
        _moba_kernel,
        grid=(B,),
        in_specs=[
            pl.BlockSpec((1, SEQ, 3 * GROUP_W), lambda b: (b, 0, 0)),
            _resident((Q_BLOCK, Q_BLOCK)),
        ],
        out_specs=pl.BlockSpec((1, SEQ, GROUP_W), lambda b: (b, 0, 0)),
        out_shape=jax.ShapeDtypeStruct((B, SEQ, GROUP_W), BF16),
        scratch_shapes=[
            pltpu.VMEM((N_HEADS, SEQ, 2 * LANES), BF16),
            pltpu.VMEM((N_HEADS, SEQ, 2 * LANES), BF16),
            pltpu.VMEM((N_HEADS, VT_ROWS, SEQ), BF16),
        ],
        compiler_params=_cparams("parallel"),
        name="moba",
    )(qkv, causal_bias_t)


def _dil_kernel(qkv_ref, bias_ref, o_ref, q_ref, k_ref, vt_ref):
    for pair in range(2):
        _store_values_t(qkv_ref[0, :, 2 * GROUP_W + pair * LANES: 2 * GROUP_W + (pair + 1) * LANES], vt_ref, pair)
    lane_head = lax.broadcasted_iota(jnp.int32, (1, LANES), 1) // HEAD_DIM
    for pair in range(2):
        q_ref[pair] = (qkv_ref[0, :, pair * LANES:(pair + 1) * LANES] * Q_SCALE).astype(BF16)
        k_slab = qkv_ref[0, :, GROUP_W + pair * LANES: GROUP_W + (pair + 1) * LANES]
        for hh in range(2):
            k_ref[2 * pair + hh] = jnp.where(lane_head == hh, k_slab, 0.0).astype(BF16)

    def make_unit(i, h, outs):
        r0 = i * Q_BLOCK
        bias0 = SEQ - (r0 + Q_BLOCK)

        def score(j):
            keys = slice(j * KEY_CHUNK, (j + 1) * KEY_CHUNK)
            s = _scores_t(k_ref[h, keys, :], q_ref[h // 2, r0:r0 + Q_BLOCK, :])
            return s + bias_ref[bias0 + j * KEY_CHUNK: bias0 + (j + 1) * KEY_CHUNK, :]

        def done(out_t):
            outs[h] = out_t
            if len(outs) == N_HEADS:
                _store_heads(o_ref, r0, [outs[hd] for hd in range(N_HEADS)])

        return _AttnUnit((i + 1) * Q_CHUNKS, score, lambda j: vt_ref[h, :, j * KEY_CHUNK:(j + 1) * KEY_CHUNK], done)

    units = []
    for i in range(SEQ // Q_BLOCK):
        outs = {}
        units += [make_unit(i, h, outs) for h in range(N_HEADS)]
    _run_interleaved(units, DIL_STREAMS)


def _dil(qkv, dist_bias_t):
    B = qkv.shape[0]
    return pl.pallas_call(
        _dil_kernel,
        grid=(B,),
        in_specs=[
            pl.BlockSpec((1, SEQ, 3 * GROUP_W), lambda b: (b, 0, 0)),
            _resident((SEQ, Q_BLOCK)),
        ],
        out_specs=pl.BlockSpec((1, SEQ, GROUP_W), lambda b: (b, 0, 0)),
        out_shape=jax.ShapeDtypeStruct((B, SEQ, GROUP_W), BF16),
        scratch_shapes=[
            pltpu.VMEM((N_HEADS // 2, SEQ, LANES), BF16),
            pltpu.VMEM((N_HEADS, SEQ, LANES), BF16),
            pltpu.VMEM((N_HEADS, VT_ROWS, SEQ), BF16),
        ],
        compiler_params=_cparams("parallel"),
        name="dilated",
    )(qkv, dist_bias_t)


def _causal_bias_t():
    r = np.arange(Q_BLOCK)
    return jnp.asarray(np.where(r[:, None] <= r[None, :], 0.0, NEG_INF), F32)


def _dilated_distance_bias_t():
    r = np.arange(Q_BLOCK)[None, :]
    x = np.arange(SEQ)[:, None]
    d = r - (x - (SEQ - Q_BLOCK))
    cnt = ((d >= 0) & (d <= 128)).astype(np.int64)
    cnt = cnt + ((d >= 0) & (d % 4 == 0) & (d <= 512))
    cnt = cnt + ((d >= 0) & (d % 16 == 0) & (d <= 2048))
    return jnp.asarray(np.where(cnt > 0, np.log2(np.maximum(cnt, 1)), NEG_INF), F32)


def _pool_blockdiag(pool_w):
    n = len(POOL_WINDOWS)
    eye = jnp.eye(n, dtype=pool_w.dtype)
    return jnp.einsum('gcd,gh->gchd', pool_w, eye).reshape(n * POOL_GROUP, n * POOL_GROUP)


def kernel(x, positions, ffn1_norm, ffn1_gate, ffn1_up, ffn1_down, mix_norm, w_in, pool_w, pool_scale,
           conv_w, conv_b, conv_ln_g, conv_ln_b, w_out, ffn2_norm, ffn2_gate, ffn2_up, ffn2_down, final_norm):
    B, S, D = x.shape
    assert (S, D) == (SEQ, D_MODEL)
    rope = [t.reshape(B * S, LANES) for t in _rope_tables(positions)]
    causal_t = _causal_bias_t()
    dist_bias_t = _dilated_distance_bias_t()
    pool_cnt = _pool_counts()
    row = lambda t: t.reshape(1, -1)
    rows = lambda t: t.reshape(DEPTH, 1, -1)
    conv_p = (conv_w, rows(conv_b), rows(conv_ln_g), rows(conv_ln_b))
    pre_mix = (ffn1_gate, ffn1_up, ffn1_down, w_in)
    post_mix = (ffn2_gate, ffn2_up, ffn2_down, w_out)
    wg, wu, wd, wi = (w[0].astype(BF16) for w in pre_mix)
    x2d = x.reshape(B * S, D)
    for l in range(DEPTH):
        x2d, u_pool, qkv_m, qkv_d, u_conv, wg, wu, wd, wo = _ffn_inproj(
            l, x2d, rows(ffn1_norm), wg, wu, wd, rows(mix_norm), wi, *rope, cast=(post_mix, l))
        seq = lambda t: t.reshape(B, S, -1)
        y_pool = _pool(seq(u_pool), pool_cnt, _pool_blockdiag(pool_w[l]).astype(BF16), row(pool_scale[l]))
        y_moba = _moba(seq(qkv_m), causal_t)
        y_dil = _dil(seq(qkv_d), dist_bias_t)
        flat = lambda t: t.reshape(B * S, GROUP_W)
        last = l == DEPTH - 1
        x2d, *nxt = _mix_ffn(l, x2d, flat(y_pool), flat(y_moba), flat(y_dil), u_conv, *conv_p, wo,
                             rows(ffn2_norm), wg, wu, wd, row(final_norm), final_norm=last,
                             cast=None if last else (pre_mix, l + 1))
        if not last:
            wg, wu, wd, wi = nxt
    return x2d.reshape(B, S, D)
```

```python
import functools
import math

import numpy as np
import jax
import jax.numpy as jnp
from jax import lax
from jax.experimental import pallas as pl
from jax.experimental.pallas import tpu as pltpu

F32 = jnp.float32
BF16 = jnp.bfloat16

D_MODEL = 1024
SEQ = 2048
DEPTH = 2
HEAD_DIM = 64
N_HEADS = 4
GROUP_W = 256
POOL_WINDOWS = (2, 4, 8, 16)
POOL_GROUP = 64
MOBA_BLOCK = 256
MOBA_NB = SEQ // MOBA_BLOCK
MOBA_TOPK = 3
CONV_KERNEL = 31
ROPE_THETA = 500000.0
ROPE_DIMS = 16
D_FF = 2816
FF_CHUNK = 256
FFN_ROWS = 512
TOKEN_LAG = 1
D_IN = 2304
RMS_EPS = 1e-6
LN_EPS = 1e-5
NEG_INF = -1e30
ATTN_SCALE = HEAD_DIM ** -0.5
LOG2E = math.log2(math.e)
Q_SCALE = ATTN_SCALE * LOG2E
Q_BLOCK = 256
KEY_CHUNK = 256
Q_CHUNKS = Q_BLOCK // KEY_CHUNK
MOBA_STREAMS = 8
DIL_STREAMS = 4
LANES = 128
SUBLANES = 8
BF16_ROWS = 16
VT_ROWS = HEAD_DIM + BF16_ROWS
CONV_PAD = 32
CONV_ROWS = 64

VMEM_LIMIT = 56 * 1024 * 1024


def _cparams(*sem):
    return pltpu.CompilerParams(dimension_semantics=sem, vmem_limit_bytes=VMEM_LIMIT)


def _rms(x, g):
    return x * lax.rsqrt(jnp.mean(x * x, axis=-1, keepdims=True) + RMS_EPS) * g


def _resident(shape):
    return pl.BlockSpec(shape, lambda *_: (0,) * len(shape), pipeline_mode=pl.Buffered(1))


def _layer(shape, l):
    return pl.BlockSpec((None,) + tuple(shape), lambda *_: (l,) + (0,) * len(shape), pipeline_mode=pl.Buffered(1))


def _zero_token(t):
    rows, cols = t.shape
    return jnp.minimum(jnp.abs(t.reshape(rows // SUBLANES, SUBLANES, cols)).max(axis=0), 0.0)


def _half_step_ffn(x, g_ref, wg_ref, wu_ref, wd_ref, hid_ref, side_work=()):
    n_up, n_down = D_FF // FF_CHUNK, D_MODEL // FF_CHUNK
    slots = [g for g in range(n_up + n_down - TOKEN_LAG) if g != n_up - 1]
    after = [[] for _ in range(n_up + n_down)]
    for k, piece in enumerate(side_work):
        after[slots[k * len(slots) // len(side_work)]].append(piece)

    def run_side_work(slot):
        token = None
        for piece in (after[slot] if slot < len(after) else ()):
            t = piece()
            token = t if token is None else token + t
        return token

    def gated(value, token):
        return value if token is None else jnp.concatenate([value[0:SUBLANES] + token, value[SUBLANES:]], axis=0)

    xn = _rms(x, g_ref[...]).astype(BF16)
    tokens = [None] * TOKEN_LAG
    for c in range(n_up):
        cols = slice(c * FF_CHUNK, (c + 1) * FF_CHUNK)
        gate = jnp.dot(xn, wg_ref[:, cols], preferred_element_type=F32)
        up = jnp.dot(xn, wu_ref[:, cols], preferred_element_type=F32)
        hid_ref[:, cols] = gated(jax.nn.silu(gate) * up, tokens.pop(0)).astype(BF16)
        tokens.append(run_side_work(c))
    outs = []
    for c in range(n_down):
        cols = slice(c * FF_CHUNK, (c + 1) * FF_CHUNK)
        y = x[:, cols] + 0.5 * jnp.dot(hid_ref[...], wd_ref[:, cols], preferred_element_type=F32)
        outs.append(gated(y, tokens.pop(0)))
        tokens.append(run_side_work(n_up + c))
    assert all(t is None for t in tokens)
    return jnp.concatenate(outs, axis=1)


def _conv_tile_pieces(u_ref, halo_fn, w_ref, b_ref, lg_ref, lb_ref, hs_ref, out_ref):
    tm = u_ref.shape[0]
    copy_rows = tm + CONV_PAD - SUBLANES
    first = CONV_PAD - (CONV_KERNEL - 1)

    def gate():
        h = u_ref[:, 0:GROUP_W] * jax.nn.sigmoid(u_ref[:, GROUP_W:])
        hs_ref[0, 0:CONV_PAD, :] = halo_fn()
        hs_ref[0, CONV_PAD:CONV_PAD + tm, :] = h
        return _zero_token(h)

    def copies(shifts):
        def piece():
            token = None
            for b in shifts:
                v = hs_ref[0, b:b + copy_rows, :]
                hs_ref[b, 0:copy_rows, :] = v
                token = _zero_token(v) if token is None else token + _zero_token(v)
            return token
        return piece

    def chunk(c):
        def piece():
            r0 = c * CONV_ROWS
            acc = None
            for j in range(CONV_KERNEL):
                a_, b_ = divmod(first + j, SUBLANES)
                term = w_ref[j:j + 1, :] * hs_ref[b_, r0 + SUBLANES * a_: r0 + SUBLANES * a_ + CONV_ROWS, :]
                acc = term if acc is None else acc + term
            y = acc + b_ref[...]
            mu = jnp.mean(y, axis=-1, keepdims=True)
            var = jnp.mean(jnp.square(y - mu), axis=-1, keepdims=True)
            z = (y - mu) * lax.rsqrt(var + LN_EPS) * lg_ref[...] + lb_ref[...]
            out = jax.nn.silu(z)
            out_ref[r0:r0 + CONV_ROWS, :] = out.astype(BF16)
            return _zero_token(out)
        return piece

    return [gate, copies((1, 2, 3, 4)), copies((5, 6, 7))] + [chunk(c) for c in range(tm // CONV_ROWS)]


def _rope(t, c, s_up, s_dn):
    half = ROPE_DIMS // 2
    return t * c + pltpu.roll(t, half, axis=1) * s_up + pltpu.roll(t, LANES - half, axis=1) * s_dn


def _cast_side_outputs(cast_in, cast_out):
    for src, dst in zip(cast_in, cast_out):
        dst[...] = src[...].astype(BF16)


def _ffn_inproj_kernel(x_ref, g_ref, wg_ref, wu_ref, wd_ref, gm_ref, wi_ref, c_ref, su_ref, sd_ref, *rest, n_cast):
    cast_in, (o_ref, pool_ref, moba_ref, dil_ref, conv_ref) = rest[:n_cast], rest[n_cast:n_cast + 5]
    cast_out, (hid_ref,) = rest[n_cast + 5:2 * n_cast + 5], rest[2 * n_cast + 5:]
    _cast_side_outputs(cast_in, cast_out)
    x = _half_step_ffn(x_ref[...], g_ref, wg_ref, wu_ref, wd_ref, hid_ref)
    o_ref[...] = x
    h = jnp.dot(_rms(x, gm_ref[...]).astype(BF16), wi_ref[...], preferred_element_type=F32)
    c, su, sd = c_ref[...], su_ref[...], sd_ref[...]
    pool_ref[...] = h[:, 0:GROUP_W]
    for out_ref, off in ((moba_ref, GROUP_W), (dil_ref, 4 * GROUP_W)):
        for part in range(6):
            piece = h[:, off + part * LANES: off + (part + 1) * LANES]
            if part < 4:
                piece = _rope(piece, c, su, sd)
            out_ref[:, part * LANES:(part + 1) * LANES] = piece
    conv_ref[...] = h[:, 7 * GROUP_W:]


def _mix_ffn_kernel(x_ref, yp_ref, ym_ref, yd_ref, u0_ref, un_ref, uh_ref, cw_ref, cb_ref, clg_ref, clb_ref,
                    wo_ref, g_ref, wg_ref, wu_ref, wd_ref, fin_ref, *rest, n_cast, final_norm):
    cast_in, o_ref, cast_out = rest[:n_cast], rest[n_cast], rest[n_cast + 1:2 * n_cast + 1]
    hid_ref, hs_ref, yc_ref = rest[2 * n_cast + 1:]
    _cast_side_outputs(cast_in, cast_out)
    i = pl.program_id(0)
    conv_refs = (cw_ref, cb_ref, clg_ref, clb_ref, hs_ref)

    @pl.when(i == 0)
    def _():
        zero_halo = lambda: jnp.zeros((CONV_PAD, GROUP_W), F32)
        for piece in _conv_tile_pieces(u0_ref, zero_halo, *conv_refs, yc_ref.at[0]):
            piece()

    def halo():
        h = uh_ref[:, 0:GROUP_W] * jax.nn.sigmoid(uh_ref[:, GROUP_W:])
        return jnp.where((i + 1) % (SEQ // FFN_ROWS) == 0, 0.0, h)

    next_conv = _conv_tile_pieces(un_ref, halo, *conv_refs, yc_ref.at[(i + 1) % 2])
    mix = jnp.concatenate([yp_ref[...], ym_ref[...], yd_ref[...], yc_ref[i % 2]], axis=1)
    x = x_ref[...] + jnp.dot(mix, wo_ref[...], preferred_element_type=F32)
    y = _half_step_ffn(x, g_ref, wg_ref, wu_ref, wd_ref, hid_ref, side_work=next_conv)
    if final_norm:
        y = _rms(y, fin_ref[...])
    o_ref[...] = y


def _ffn_weight_specs(l):
    return [_layer((1, D_MODEL), l), _resident((D_MODEL, D_FF)), _resident((D_MODEL, D_FF)),
            _resident((D_FF, D_MODEL))]


def _rows(width, tm):
    return pl.BlockSpec((tm, width), lambda i: (i, 0))


def _cast_specs(cast, n_steps):
    in_specs, out_specs, out_shapes, operands = [], [], [], []
    weights, layer = cast if cast is not None else ((), 0)
    for w in weights:
        _, rows, cols = w.shape
        rb = next(r for r in range(BF16_ROWS, rows + 1, BF16_ROWS) if rows % r == 0 and rows // r <= n_steps)
        last = rows // rb - 1
        in_specs.append(pl.BlockSpec((None, rb, cols), lambda i, last=last: (layer, jnp.minimum(i, last), 0)))
        out_specs.append(pl.BlockSpec((rb, cols), lambda i, last=last: (jnp.minimum(i, last), 0)))
        out_shapes.append(jax.ShapeDtypeStruct((rows, cols), BF16))
        operands.append(w)
    return in_specs, out_specs, out_shapes, operands


def _ffn_inproj(l, x2d, norm_g, wg, wu, wd, mix_g, w_in, rope_c, rope_su, rope_sd, *, cast=None, tm=FFN_ROWS):
    n_tok = x2d.shape[0]
    widths = (D_MODEL, GROUP_W, 3 * GROUP_W, 3 * GROUP_W, 2 * GROUP_W)
    c_in, c_out, c_shapes, c_ops = _cast_specs(cast, n_tok // tm)
    return pl.pallas_call(
        functools.partial(_ffn_inproj_kernel, n_cast=len(c_ops)),
        grid=(n_tok // tm,),
        in_specs=[_rows(D_MODEL, tm)] + _ffn_weight_specs(l)
                 + [_layer((1, D_MODEL), l), _resident((D_MODEL, D_IN))] + [_rows(LANES, tm)] * 3 + c_in,
        out_specs=[_rows(w, tm) for w in widths] + c_out,
        out_shape=[jax.ShapeDtypeStruct((n_tok, w), F32) for w in widths] + c_shapes,
        scratch_shapes=[pltpu.VMEM((tm, D_FF), BF16)],
        compiler_params=_cparams("arbitrary"),
        name="ffn_inproj",
    )(x2d, norm_g, wg, wu, wd, mix_g, w_in, rope_c, rope_su, rope_sd, *c_ops)


def _mix_ffn(l, x2d, yp, ym, yd, u_conv, conv_w, conv_b, conv_lg, conv_lb, w_out, norm_g, wg, wu, wd, fin_g,
             *, final_norm, cast=None, tm=FFN_ROWS):
    n_tok = x2d.shape[0]
    n_tiles = n_tok // tm
    halo_blocks = tm // CONV_PAD
    u_first = pl.BlockSpec((tm, 2 * GROUP_W), lambda i: (0, 0))
    u_next = pl.BlockSpec((tm, 2 * GROUP_W), lambda i: (jnp.minimum(i + 1, n_tiles - 1), 0))
    u_halo = pl.BlockSpec((CONV_PAD, 2 * GROUP_W),
                          lambda i: (jnp.minimum((i + 1) * halo_blocks - 1, n_tiles * halo_blocks - 1), 0))
    c_in, c_out, c_shapes, c_ops = _cast_specs(cast, n_tiles)
    return pl.pallas_call(
        functools.partial(_mix_ffn_kernel, n_cast=len(c_ops), final_norm=final_norm),
        grid=(n_tiles,),
        in_specs=[_rows(D_MODEL, tm)] + [_rows(GROUP_W, tm)] * 3 + [u_first, u_next, u_halo]
                 + [_layer((CONV_KERNEL, GROUP_W), l)] + [_layer((1, GROUP_W), l)] * 3
                 + [_resident((D_MODEL, D_MODEL))] + _ffn_weight_specs(l) + [_resident((1, D_MODEL))] + c_in,
        out_specs=[_rows(D_MODEL, tm)] + c_out,
        out_shape=[jax.ShapeDtypeStruct((n_tok, D_MODEL), F32)] + c_shapes,
        scratch_shapes=[pltpu.VMEM((tm, D_FF), BF16),
                        pltpu.VMEM((SUBLANES, tm + CONV_PAD, GROUP_W), F32),
                        pltpu.VMEM((2, tm, GROUP_W), BF16)],
        compiler_params=_cparams("arbitrary"),
        name="mix_ffn",
    )(x2d, yp, ym, yd, u_conv, u_conv, u_conv, conv_w, conv_b, conv_lg, conv_lb, w_out, norm_g, wg, wu, wd, fin_g,
      *c_ops)


def _rope_table_kernel(pos_ref, inv_ref, c_ref, su_ref, sd_ref):
    half = ROPE_DIMS // 2
    ang = pos_ref[0].astype(F32) * inv_ref[...]
    cos, sin = jnp.cos(ang), jnp.sin(ang)
    zeros = lambda n: jnp.zeros((n, SEQ), F32)
    rest = HEAD_DIM - ROPE_DIMS
    c = jnp.concatenate([cos, cos, jnp.ones((rest, SEQ), F32)] * 2, axis=0)
    s_up = jnp.concatenate([zeros(half), sin, zeros(rest)] * 2, axis=0)
    s_dn = jnp.concatenate([-sin, zeros(half), zeros(rest)] * 2, axis=0)
    c_ref[0] = c.T
    su_ref[0] = s_up.T
    sd_ref[0] = s_dn.T


def _rope_tables(positions):
    B = positions.shape[0]
    half = ROPE_DIMS // 2
    inv = ROPE_THETA ** (-jnp.arange(0, ROPE_DIMS, 2, dtype=F32) / ROPE_DIMS)
    table = jax.ShapeDtypeStruct((B, SEQ, LANES), F32)
    out_spec = pl.BlockSpec((1, SEQ, LANES), lambda b: (b, 0, 0))
    return pl.pallas_call(
        _rope_table_kernel,
        grid=(B,),
        in_specs=[pl.BlockSpec((1, 1, SEQ), lambda b: (b, 0, 0)), _resident((half, 1))],
        out_specs=[out_spec] * 3,
        out_shape=[table] * 3,
        compiler_params=_cparams("parallel"),
        name="rope_tables",
    )(positions.reshape(B, 1, SEQ), inv.reshape(half, 1))


def _pool_kernel(u_ref, cnt_ref, w_ref, scale_ref, o_ref):
    row = lax.broadcasted_iota(jnp.int32, (SEQ, LANES), 0)
    low_group = lax.broadcasted_iota(jnp.int32, (SEQ, LANES), 1) < POOL_GROUP

    def shifted(x, s):
        return jnp.where(row >= s, pltpu.roll(x, s, axis=0), 0.0)

    pooled = []
    for c in range(GROUP_W // LANES):
        lanes = slice(c * LANES, (c + 1) * LANES)
        u = u_ref[0, :, lanes]
        lo, hi = POOL_WINDOWS[2 * c], POOL_WINDOWS[2 * c + 1]
        sums, w = {1: u}, 1
        while w < hi:
            sums[2 * w] = sums[w] + shifted(sums[w], w)
            w *= 2
        wsum = jnp.where(low_group, sums[lo], sums[hi])
        pooled.append((wsum / cnt_ref[:, lanes] - u).astype(BF16))
    mixed = jnp.dot(jnp.concatenate(pooled, axis=1), w_ref[...], preferred_element_type=F32)
    o_ref[0] = (mixed * scale_ref[...]).astype(BF16)


def _pool_counts():
    t = np.arange(SEQ)[:, None] + 1
    wnd = np.repeat(np.asarray(POOL_WINDOWS), POOL_GROUP)[None, :]
    return jnp.asarray(np.minimum(t, wnd), F32)


def _pool(u, counts, w_blockdiag, scale):
    B = u.shape[0]
    return pl.pallas_call(
        _pool_kernel,
        grid=(B,),
        in_specs=[
            pl.BlockSpec((1, SEQ, GROUP_W), lambda b: (b, 0, 0)),
            _resident((SEQ, GROUP_W)),
            _resident((GROUP_W, GROUP_W)),
            _resident((1, GROUP_W)),
        ],
        out_specs=pl.BlockSpec((1, SEQ, GROUP_W), lambda b: (b, 0, 0)),
        out_shape=jax.ShapeDtypeStruct((B, SEQ, GROUP_W), BF16),
        compiler_params=_cparams("parallel"),
        name="pool",
    )(u, counts, w_blockdiag, scale)


def _scores_t(k, q):
    return lax.dot_general(k, q, (((1,), (1,)), ((), ())), preferred_element_type=F32)


class _AttnUnit:
    def __init__(self, n_chunks, score_fn, vt_fn, done_fn):
        self.n_chunks, self.score_fn, self.vt_fn, self.done_fn = n_chunks, score_fn, vt_fn, done_fn
        self.m = None
        self.acc = None
        self.scores = {}

    def score(self, j):
        s = self.score_fn(j)
        self.scores[j] = s
        cm = s.max(axis=0, keepdims=True)
        self.m = cm if self.m is None else jnp.maximum(self.m, cm)

    def apply(self, j):
        p = jnp.exp2(self.scores.pop(j) - self.m).astype(BF16)
        pv = jnp.dot(self.vt_fn(j), p, preferred_element_type=F32)
        self.acc = pv if self.acc is None else self.acc + pv

    def finish(self):
        self.done_fn(self.acc[0:HEAD_DIM] / self.acc[HEAD_DIM:HEAD_DIM + 1])


def _pipeline(units):
    prev = None
    for unit in list(units) + [None]:
        n_score = unit.n_chunks if unit is not None else 0
        n_apply = prev.n_chunks if prev is not None else 0
        for j in range(max(n_score, n_apply)):
            if j < n_score:
                unit.score(j)
            if j < n_apply:
                prev.apply(j)
            yield
        if prev is not None:
            prev.finish()
        prev = unit


def _run_interleaved(units, streams=1):
    live = [_pipeline(units[s::streams]) for s in range(streams)]
    while live:
        live = [g for g in live if next(g, StopIteration) is not StopIteration]


def _store_values_t(v_slab_ref_slice, vt_ref, pair):
    vt = v_slab_ref_slice.T
    ones_blk = (lax.broadcasted_iota(jnp.int32, (BF16_ROWS, SEQ), 0) == 0).astype(BF16)
    for hh in range(2):
        h = 2 * pair + hh
        vt_ref[h, 0:HEAD_DIM, :] = vt[hh * HEAD_DIM:(hh + 1) * HEAD_DIM].astype(BF16)
        vt_ref[h, HEAD_DIM:VT_ROWS, :] = ones_blk


def _store_heads(o_ref, r0, outs_t):
    for pair in range(2):
        both = jnp.concatenate(outs_t[2 * pair:2 * pair + 2], axis=0)
        o_ref[0, r0:r0 + Q_BLOCK, pair * LANES:(pair + 1) * LANES] = both.T.astype(BF16)


def _moba_kernel(qkv_ref, causal_ref, o_ref, qa_ref, ka_ref, vt_ref):
    S = SEQ
    blk_row = lax.broadcasted_iota(jnp.int32, (MOBA_NB, S), 0)
    q_blk = lax.broadcasted_iota(jnp.int32, (MOBA_NB, S), 1) // MOBA_BLOCK
    past = blk_row < q_blk
    key_blk = lax.broadcasted_iota(jnp.int32, (S, LANES), 0) // MOBA_BLOCK
    onehot = (lax.broadcasted_iota(jnp.int32, (S, LANES), 1) == key_blk).astype(BF16)
    lane_head = lax.broadcasted_iota(jnp.int32, (1, LANES), 1) // HEAD_DIM

    def split(t):
        hi = t.astype(BF16)
        return hi, (t - hi.astype(F32)).astype(BF16)

    for pair in range(2):
        _store_values_t(qkv_ref[0, :, 2 * GROUP_W + pair * LANES: 2 * GROUP_W + (pair + 1) * LANES], vt_ref, pair)
        q_slab = qkv_ref[0, :, pair * LANES:(pair + 1) * LANES] * Q_SCALE
        k_slab = qkv_ref[0, :, GROUP_W + pair * LANES: GROUP_W + (pair + 1) * LANES]
        kmean = jnp.mean(k_slab.reshape(MOBA_NB, MOBA_BLOCK, LANES), axis=1)
        kmean2 = jnp.concatenate([jnp.where(lane_head == hh, kmean, 0.0) for hh in range(2)], axis=0)
        q_hi, q_lo = split(q_slab)
        k_hi, k_lo = split(kmean2)
        gate2 = _scores_t(jnp.concatenate([k_hi, k_lo, k_hi], axis=1),
                          jnp.concatenate([q_hi, q_hi, q_lo], axis=1))
        for hh in range(2):
            h = 2 * pair + hh
            gate = jnp.where(past, gate2[hh * MOBA_NB:(hh + 1) * MOBA_NB], NEG_INF)
            rank = jnp.zeros((MOBA_NB, S), jnp.int32)
            for jp in range(MOBA_NB):
                gj = gate[jp:jp + 1, :]
                beats = (gj > gate) | ((gj == gate) & (jp < blk_row))
                rank = rank + beats.astype(jnp.int32)
            keep = (past & (rank < MOBA_TOPK)) | (blk_row == q_blk)
            bias_t = jnp.where(keep, 0.0, NEG_INF)
            bias_t = jnp.concatenate([bias_t, jnp.zeros((LANES - MOBA_NB, S), F32)], axis=0)
            qa_ref[h] = jnp.concatenate([q_hi, bias_t.T.astype(BF16)], axis=1)
            ka_ref[h] = jnp.concatenate([jnp.where(lane_head == hh, k_slab, 0.0).astype(BF16), onehot], axis=1)

    def make_unit(i, h, outs):
        r0 = i * Q_BLOCK

        def score(j):
            s = _scores_t(ka_ref[h, j * KEY_CHUNK:(j + 1) * KEY_CHUNK, :], qa_ref[h, r0:r0 + Q_BLOCK, :])
            d = j - i * Q_CHUNKS
            return s + causal_ref[d * KEY_CHUNK:(d + 1) * KEY_CHUNK, :] if d >= 0 else s

        def done(out_t):
            outs[h] = out_t
            if len(outs) == N_HEADS:
                _store_heads(o_ref, r0, [outs[hd] for hd in range(N_HEADS)])

        return _AttnUnit((i + 1) * Q_CHUNKS, score, lambda j: vt_ref[h, :, j * KEY_CHUNK:(j + 1) * KEY_CHUNK], done)

    units = []
    for i in range(SEQ // Q_BLOCK):
        outs = {}
        units += [make_unit(i, h, outs) for h in range(N_HEADS)]
    _run_interleaved(units, MOBA_STREAMS)


def _moba(qkv, causal_bias_t):
    B = qkv.shape[0]
    return pl.pallas_call(
        _moba_kernel,
        grid=(B,),
        in_specs=[
            pl.BlockSpec((1, SEQ, 3 * GROUP_W), lambda b: (b, 0, 0)),
            _resident((Q_BLOCK, Q_BLOCK)),
        ],
        out_specs=pl.BlockSpec((1, SEQ, GROUP_W), lambda b: (b, 0, 0)),
        out_shape=jax.ShapeDtypeStruct((B, SEQ, GROUP_W), BF16),
        scratch_shapes=[
            pltpu.VMEM((N_HEADS, SEQ, 2 * LANES), BF16),
            pltpu.VMEM((N_HEADS, SEQ, 2 * LANES), BF16),
            pltpu.VMEM((N_HEADS, VT_ROWS, SEQ), BF16),
        ],
        compiler_params=_cparams("parallel"),
        name="moba",
    )(qkv, causal_bias_t)


def _dil_kernel(qkv_ref, bias_ref, o_ref, q_ref, k_ref, vt_ref):
    for pair in range(2):
        _store_values_t(qkv_ref[0, :, 2 * GROUP_W + pair * LANES: 2 * GROUP_W + (pair + 1) * LANES], vt_ref, pair)
    lane_head = lax.broadcasted_iota(jnp.int32, (1, LANES), 1) // HEAD_DIM
    for pair in range(2):
        q_ref[pair] = (qkv_ref[0, :, pair * LANES:(pair + 1) * LANES] * Q_SCALE).astype(BF16)
        k_slab = qkv_ref[0, :, GROUP_W + pair * LANES: GROUP_W + (pair + 1) * LANES]
        for hh in range(2):
            k_ref[2 * pair + hh] = jnp.where(lane_head == hh, k_slab, 0.0).astype(BF16)

    def make_unit(i, h, outs):
        r0 = i * Q_BLOCK
        bias0 = SEQ - (r0 + Q_BLOCK)

        def score(j):
            keys = slice(j * KEY_CHUNK, (j + 1) * KEY_CHUNK)
            s = _scores_t(k_ref[h, keys, :], q_ref[h // 2, r0:r0 + Q_BLOCK, :])
            return s + bias_ref[bias0 + j * KEY_CHUNK: bias0 + (j + 1) * KEY_CHUNK, :]

        def done(out_t):
            outs[h] = out_t
            if len(outs) == N_HEADS:
                _store_heads(o_ref, r0, [outs[hd] for hd in range(N_HEADS)])

        return _AttnUnit((i + 1) * Q_CHUNKS, score, lambda j: vt_ref[h, :, j * KEY_CHUNK:(j + 1) * KEY_CHUNK], done)

    units = []
    for i in range(SEQ // Q_BLOCK):
        outs = {}
        units += [make_unit(i, h, outs) for h in range(N_HEADS)]
    _run_interleaved(units, DIL_STREAMS)


def _dil(qkv, dist_bias_t):
    B = qkv.shape[0]
    return pl.pallas_call(
        _dil_kernel,
        grid=(B,),
        in_specs=[
            pl.BlockSpec((1, SEQ, 3 * GROUP_W), lambda b: (b, 0, 0)),
            _resident((SEQ, Q_BLOCK)),
        ],
        out_specs=pl.BlockSpec((1, SEQ, GROUP_W), lambda b: (b, 0, 0)),
        out_shape=jax.ShapeDtypeStruct((B, SEQ, GROUP_W), BF16),
        scratch_shapes=[
            pltpu.VMEM((N_HEADS // 2, SEQ, LANES), BF16),
            pltpu.VMEM((N_HEADS, SEQ, LANES), BF16),
            pltpu.VMEM((N_HEADS, VT_ROWS, SEQ), BF16),
        ],
        compiler_params=_cparams("parallel"),
        name="dilated",
    )(qkv, dist_bias_t)


def _causal_bias_t():
    r = np.arange(Q_BLOCK)
    return jnp.asarray(np.where(r[:, None] <= r[None, :], 0.0, NEG_INF), F32)


def _dilated_distance_bias_t():
    r = np.arange(Q_BLOCK)[None, :]
    x = np.arange(SEQ)[:, None]
    d = r - (x - (SEQ - Q_BLOCK))
    cnt = ((d >= 0) & (d <= 128)).astype(np.int64)
    cnt = cnt + ((d >= 0) & (d % 4 == 0) & (d <= 512))
    cnt = cnt + ((d >= 0) & (d % 16 == 0) & (d <= 2048))
    return jnp.asarray(np.where(cnt > 0, np.log2(np.maximum(cnt, 1)), NEG_INF), F32)


def _pool_blockdiag(pool_w):
    n = len(POOL_WINDOWS)
    eye = jnp.eye(n, dtype=pool_w.dtype)
    return jnp.einsum('gcd,gh->gchd', pool_w, eye).reshape(n * POOL_GROUP, n * POOL_GROUP)


def kernel(x, positions, ffn1_norm, ffn1_gate, ffn1_up, ffn1_down, mix_norm, w_in, pool_w, pool_scale,
           conv_w, conv_b, conv_ln_g, conv_ln_b, w_out, ffn2_norm, ffn2_gate, ffn2_up, ffn2_down, final_norm):
    B, S, D = x.shape
    assert (S, D) == (SEQ, D_MODEL)
    rope = [t.reshape(B * S, LANES) for t in _rope_tables(positions)]
    causal_t = _causal_bias_t()
    dist_bias_t = _dilated_distance_bias_t()
    pool_cnt = _pool_counts()
    row = lambda t: t.reshape(1, -1)
    rows = lambda t: t.reshape(DEPTH, 1, -1)
    conv_p = (conv_w, rows(conv_b), rows(conv_ln_g), rows(conv_ln_b))
    pre_mix = (ffn1_gate, ffn1_up, ffn1_down, w_in)
    post_mix = (ffn2_gate, ffn2_up, ffn2_down, w_out)
    wg, wu, wd, wi = (w[0].astype(BF16) for w in pre_mix)
    x2d = x.reshape(B * S, D)
    for l in range(DEPTH):
        x2d, u_pool, qkv_m, qkv_d, u_conv, wg, wu, wd, wo = _ffn_inproj(
            l, x2d, rows(ffn1_norm), wg, wu, wd, rows(mix_norm), wi, *rope, cast=(post_mix, l))
        seq = lambda t: t.reshape(B, S, -1)
        y_pool = _pool(seq(u_pool), pool_cnt, _pool_blockdiag(pool_w[l]).astype(BF16), row(pool_scale[l]))
        y_moba = _moba(seq(qkv_m), causal_t)
        y_dil = _dil(seq(qkv_d), dist_bias_t)
        flat = lambda t: t.reshape(B * S, GROUP_W)
        last = l == DEPTH - 1
        x2d, *nxt = _mix_ffn(l, x2d, flat(y_pool), flat(y_moba), flat(y_dil), u_conv, *conv_p, wo,
                             rows(ffn2_norm), wg, wu, wd, row(final_norm), final_norm=last,
                             cast=None if last else (pre_mix, l + 1))
        if not last:
            wg, wu, wd, wi = nxt
    return x2d.reshape(B, S, D)
```

```python
import functools
import math

import numpy as np
import jax
import jax.numpy as jnp
from jax import lax
from jax.experimental import pallas as pl
from jax.experimental.pallas import tpu as pltpu

F32 = jnp.float32
BF16 = jnp.bfloat16

D_MODEL = 1024
SEQ = 2048
DEPTH = 2
HEAD_DIM = 64
N_HEADS = 4
GROUP_W = 256
POOL_WINDOWS = (2, 4, 8, 16)
POOL_GROUP = 64
MOBA_BLOCK = 256
MOBA_NB = SEQ // MOBA_BLOCK
MOBA_TOPK = 3
CONV_KERNEL = 31
ROPE_THETA = 500000.0
ROPE_DIMS = 16
D_FF = 2816
FF_CHUNK = 256
FFN_ROWS = 512
TOKEN_LAG = 1
D_IN = 2304
DIL_QK_W = (2 + N_HEADS) * 128
N_OUT = 6
RMS_EPS = 1e-6
LN_EPS = 1e-5
NEG_INF = -1e30
ATTN_SCALE = HEAD_DIM ** -0.5
LOG2E = math.log2(math.e)
Q_SCALE = ATTN_SCALE * LOG2E
Q_BLOCK = 256
KEY_CHUNK = 256
Q_CHUNKS = Q_BLOCK // KEY_CHUNK
MOBA_STREAMS = 8
DIL_STREAMS = 4
LANES = 128
SUBLANES = 8
BF16_ROWS = 16
VT_ROWS = HEAD_DIM + BF16_ROWS
CONV_PAD = 32
CONV_ROWS = 64

VMEM_LIMIT = 56 * 1024 * 1024


def _cparams(*sem):
    return pltpu.CompilerParams(dimension_semantics=sem, vmem_limit_bytes=VMEM_LIMIT)


def _rms(x, g):
    return x * lax.rsqrt(jnp.mean(x * x, axis=-1, keepdims=True) + RMS_EPS) * g


def _resident(shape):
    return pl.BlockSpec(shape, lambda *_: (0,) * len(shape), pipeline_mode=pl.Buffered(1))


def _layer(shape, l):
    return pl.BlockSpec((None,) + tuple(shape), lambda *_: (l,) + (0,) * len(shape), pipeline_mode=pl.Buffered(1))


def _zero_token(t):
    rows, cols = t.shape
    return jnp.minimum(jnp.abs(t.reshape(rows // SUBLANES, SUBLANES, cols)).max(axis=0), 0.0)


def _half_step_ffn(x, g_ref, wg_ref, wu_ref, wd_ref, hid_ref, side_work=()):
    n_up, n_down = D_FF // FF_CHUNK, D_MODEL // FF_CHUNK
    slots = [g for g in range(n_up + n_down - TOKEN_LAG) if g != n_up - 1]
    after = [[] for _ in range(n_up + n_down)]
    for k, piece in enumerate(side_work):
        after[slots[k * len(slots) // len(side_work)]].append(piece)

    def run_side_work(slot):
        token = None
        for piece in (after[slot] if slot < len(after) else ()):
            t = piece()
            token = t if token is None else token + t
        return token

    def gated(value, token):
        return value if token is None else jnp.concatenate([value[0:SUBLANES] + token, value[SUBLANES:]], axis=0)

    xn = _rms(x, g_ref[...]).astype(BF16)
    tokens = [None] * TOKEN_LAG
    for c in range(n_up):
        cols = slice(c * FF_CHUNK, (c + 1) * FF_CHUNK)
        gate = jnp.dot(xn, wg_ref[:, cols], preferred_element_type=F32)
        up = jnp.dot(xn, wu_ref[:, cols], preferred_element_type=F32)
        hid_ref[:, cols] = gated(jax.nn.silu(gate) * up, tokens.pop(0)).astype(BF16)
        tokens.append(run_side_work(c))
    outs = []
    for c in range(n_down):
        cols = slice(c * FF_CHUNK, (c + 1) * FF_CHUNK)
        y = x[:, cols] + 0.5 * jnp.dot(hid_ref[...], wd_ref[:, cols], preferred_element_type=F32)
        outs.append(gated(y, tokens.pop(0)))
        tokens.append(run_side_work(n_up + c))
    assert all(t is None for t in tokens)
    return jnp.concatenate(outs, axis=1)


def _conv_tile_pieces(u_ref, halo_fn, w_ref, b_ref, lg_ref, lb_ref, hs_ref, out_ref):
    tm = u_ref.shape[0]
    copy_rows = tm + CONV_PAD - SUBLANES
    first = CONV_PAD - (CONV_KERNEL - 1)

    def gate():
        h = u_ref[:, 0:GROUP_W] * jax.nn.sigmoid(u_ref[:, GROUP_W:])
        hs_ref[0, 0:CONV_PAD, :] = halo_fn()
        hs_ref[0, CONV_PAD:CONV_PAD + tm, :] = h
        return _zero_token(h)

    def copies(shifts):
        def piece():
            token = None
            for b in shifts:
                v = hs_ref[0, b:b + copy_rows, :]
                hs_ref[b, 0:copy_rows, :] = v
                token = _zero_token(v) if token is None else token + _zero_token(v)
            return token
        return piece

    def chunk(c):
        def piece():
            r0 = c * CONV_ROWS
            acc = None
            for j in range(CONV_KERNEL):
                a_, b_ = divmod(first + j, SUBLANES)
                term = w_ref[j:j + 1, :] * hs_ref[b_, r0 + SUBLANES * a_: r0 + SUBLANES * a_ + CONV_ROWS, :]
                acc = term if acc is None else acc + term
            y = acc + b_ref[...]
            mu = jnp.mean(y, axis=-1, keepdims=True)
            var = jnp.mean(jnp.square(y - mu), axis=-1, keepdims=True)
            z = (y - mu) * lax.rsqrt(var + LN_EPS) * lg_ref[...] + lb_ref[...]
            out = jax.nn.silu(z)
            out_ref[r0:r0 + CONV_ROWS, :] = out.astype(BF16)
            return _zero_token(out)
        return piece

    return [gate, copies((1, 2, 3, 4)), copies((5, 6, 7))] + [chunk(c) for c in range(tm // CONV_ROWS)]


def _rope(t, c, s_up, s_dn):
    half = ROPE_DIMS // 2
    return t * c + pltpu.roll(t, half, axis=1) * s_up + pltpu.roll(t, LANES - half, axis=1) * s_dn


def _cast_side_outputs(cast_in, cast_out):
    for src, dst in zip(cast_in, cast_out):
        dst[...] = src[...].astype(BF16)


def _ffn_inproj_kernel(x_ref, g_ref, wg_ref, wu_ref, wd_ref, gm_ref, wi_ref, c_ref, su_ref, sd_ref, *rest, n_cast):
    cast_in, (o_ref, pool_ref, moba_ref, dqk_ref, dv_ref, conv_ref) = rest[:n_cast], rest[n_cast:n_cast + N_OUT]
    cast_out, (hid_ref,) = rest[n_cast + N_OUT:2 * n_cast + N_OUT], rest[2 * n_cast + N_OUT:]
    _cast_side_outputs(cast_in, cast_out)
    x = _half_step_ffn(x_ref[...], g_ref, wg_ref, wu_ref, wd_ref, hid_ref)
    o_ref[...] = x
    h = jnp.dot(_rms(x, gm_ref[...]).astype(BF16), wi_ref[...], preferred_element_type=F32)
    c, su, sd = c_ref[...], su_ref[...], sd_ref[...]
    pool_ref[...] = h[:, 0:GROUP_W]
    lane_head = lax.broadcasted_iota(jnp.int32, (1, LANES), 1) // HEAD_DIM
    for part in range(6):
        lanes = slice(part * LANES, (part + 1) * LANES)
        m_piece = h[:, GROUP_W + part * LANES: GROUP_W + (part + 1) * LANES]
        d_piece = h[:, 4 * GROUP_W + part * LANES: 4 * GROUP_W + (part + 1) * LANES]
        if part < 4:
            m_piece, d_piece = _rope(m_piece, c, su, sd), _rope(d_piece, c, su, sd)
        moba_ref[:, lanes] = m_piece
        if part < 2:
            dqk_ref[:, lanes] = (d_piece * Q_SCALE).astype(BF16)
        elif part < 4:
            for hh in range(2):
                head = 2 * (part - 2) + hh
                dqk_ref[:, GROUP_W + head * LANES: GROUP_W + (head + 1) * LANES] = (
                    jnp.where(lane_head == hh, d_piece, 0.0).astype(BF16))
        else:
            dv_ref[:, (part - 4) * LANES:(part - 3) * LANES] = d_piece
    conv_ref[...] = h[:, 7 * GROUP_W:]


def _mix_ffn_kernel(x_ref, yp_ref, ym_ref, yd_ref, u0_ref, un_ref, uh_ref, cw_ref, cb_ref, clg_ref, clb_ref,
                    wo_ref, g_ref, wg_ref, wu_ref, wd_ref, fin_ref, *rest, n_cast, final_norm):
    cast_in, o_ref, cast_out = rest[:n_cast], rest[n_cast], rest[n_cast + 1:2 * n_cast + 1]
    hid_ref, hs_ref, yc_ref = rest[2 * n_cast + 1:]
    _cast_side_outputs(cast_in, cast_out)
    i = pl.program_id(0)
    conv_refs = (cw_ref, cb_ref, clg_ref, clb_ref, hs_ref)

    @pl.when(i == 0)
    def _():
        zero_halo = lambda: jnp.zeros((CONV_PAD, GROUP_W), F32)
        for piece in _conv_tile_pieces(u0_ref, zero_halo, *conv_refs, yc_ref.at[0]):
            piece()

    def halo():
        h = uh_ref[:, 0:GROUP_W] * jax.nn.sigmoid(uh_ref[:, GROUP_W:])
        return jnp.where((i + 1) % (SEQ // FFN_ROWS) == 0, 0.0, h)

    next_conv = _conv_tile_pieces(un_ref, halo, *conv_refs, yc_ref.at[(i + 1) % 2])
    mix = jnp.concatenate([yp_ref[...], ym_ref[...], yd_ref[...], yc_ref[i % 2]], axis=1)
    x = x_ref[...] + jnp.dot(mix, wo_ref[...], preferred_element_type=F32)
    y = _half_step_ffn(x, g_ref, wg_ref, wu_ref, wd_ref, hid_ref, side_work=next_conv)
    if final_norm:
        y = _rms(y, fin_ref[...])
    o_ref[...] = y


def _ffn_weight_specs(l):
    return [_layer((1, D_MODEL), l), _resident((D_MODEL, D_FF)), _resident((D_MODEL, D_FF)),
            _resident((D_FF, D_MODEL))]


def _rows(width, tm):
    return pl.BlockSpec((tm, width), lambda i: (i, 0))


def _cast_specs(cast, n_steps):
    in_specs, out_specs, out_shapes, operands = [], [], [], []
    weights, layer = cast if cast is not None else ((), 0)
    for w in weights:
        _, rows, cols = w.shape
        rb = next(r for r in range(BF16_ROWS, rows + 1, BF16_ROWS) if rows % r == 0 and rows // r <= n_steps)
        last = rows // rb - 1
        in_specs.append(pl.BlockSpec((None, rb, cols), lambda i, last=last: (layer, jnp.minimum(i, last), 0)))
        out_specs.append(pl.BlockSpec((rb, cols), lambda i, last=last: (jnp.minimum(i, last), 0)))
        out_shapes.append(jax.ShapeDtypeStruct((rows, cols), BF16))
        operands.append(w)
    return in_specs, out_specs, out_shapes, operands


def _ffn_inproj(l, x2d, norm_g, wg, wu, wd, mix_g, w_in, rope_c, rope_su, rope_sd, *, cast=None, tm=FFN_ROWS):
    n_tok = x2d.shape[0]
    widths = (D_MODEL, GROUP_W, 3 * GROUP_W, DIL_QK_W, GROUP_W, 2 * GROUP_W)
    dtypes = (F32, F32, F32, BF16, F32, F32)
    assert len(widths) == N_OUT
    c_in, c_out, c_shapes, c_ops = _cast_specs(cast, n_tok // tm)
    return pl.pallas_call(
        functools.partial(_ffn_inproj_kernel, n_cast=len(c_ops)),
        grid=(n_tok // tm,),
        in_specs=[_rows(D_MODEL, tm)] + _ffn_weight_specs(l)
                 + [_layer((1, D_MODEL), l), _resident((D_MODEL, D_IN))] + [_rows(LANES, tm)] * 3 + c_in,
        out_specs=[_rows(w, tm) for w in widths] + c_out,
        out_shape=[jax.ShapeDtypeStruct((n_tok, w), dt) for w, dt in zip(widths, dtypes)] + c_shapes,
        scratch_shapes=[pltpu.VMEM((tm, D_FF), BF16)],
        compiler_params=_cparams("arbitrary"),
        name="ffn_inproj",
    )(x2d, norm_g, wg, wu, wd, mix_g, w_in, rope_c, rope_su, rope_sd, *c_ops)


def _mix_ffn(l, x2d, yp, ym, yd, u_conv, conv_w, conv_b, conv_lg, conv_lb, w_out, norm_g, wg, wu, wd, fin_g,
             *, final_norm, cast=None, tm=FFN_ROWS):
    n_tok = x2d.shape[0]
    n_tiles = n_tok // tm
    halo_blocks = tm // CONV_PAD
    u_first = pl.BlockSpec((tm, 2 * GROUP_W), lambda i: (0, 0))
    u_next = pl.BlockSpec((tm, 2 * GROUP_W), lambda i: (jnp.minimum(i + 1, n_tiles - 1), 0))
    u_halo = pl.BlockSpec((CONV_PAD, 2 * GROUP_W),
                          lambda i: (jnp.minimum((i + 1) * halo_blocks - 1, n_tiles * halo_blocks - 1), 0))
    c_in, c_out, c_shapes, c_ops = _cast_specs(cast, n_tiles)
    return pl.pallas_call(
        functools.partial(_mix_ffn_kernel, n_cast=len(c_ops), final_norm=final_norm),
        grid=(n_tiles,),
        in_specs=[_rows(D_MODEL, tm)] + [_rows(GROUP_W, tm)] * 3 + [u_first, u_next, u_halo]
                 + [_layer((CONV_KERNEL, GROUP_W), l)] + [_layer((1, GROUP_W), l)] * 3
                 + [_resident((D_MODEL, D_MODEL))] + _ffn_weight_specs(l) + [_resident((1, D_MODEL))] + c_in,
        out_specs=[_rows(D_MODEL, tm)] + c_out,
        out_shape=[jax.ShapeDtypeStruct((n_tok, D_MODEL), F32)] + c_shapes,
        scratch_shapes=[pltpu.VMEM((tm, D_FF), BF16),
                        pltpu.VMEM((SUBLANES, tm + CONV_PAD, GROUP_W), F32),
                        pltpu.VMEM((2, tm, GROUP_W), BF16)],
        compiler_params=_cparams("arbitrary"),
        name="mix_ffn",
    )(x2d, yp, ym, yd, u_conv, u_conv, u_conv, conv_w, conv_b, conv_lg, conv_lb, w_out, norm_g, wg, wu, wd, fin_g,
      *c_ops)


def _rope_table_kernel(pos_ref, inv_ref, c_ref, su_ref, sd_ref):
    half = ROPE_DIMS // 2
    ang = pos_ref[0].astype(F32) * inv_ref[...]
    cos, sin = jnp.cos(ang), jnp.sin(ang)
    zeros = lambda n: jnp.zeros((n, SEQ), F32)
    rest = HEAD_DIM - ROPE_DIMS
    c = jnp.concatenate([cos, cos, jnp.ones((rest, SEQ), F32)] * 2, axis=0)
    s_up = jnp.concatenate([zeros(half), sin, zeros(rest)] * 2, axis=0)
    s_dn = jnp.concatenate([-sin, zeros(half), zeros(rest)] * 2, axis=0)
    c_ref[0] = c.T
    su_ref[0] = s_up.T
    sd_ref[0] = s_dn.T


def _rope_tables(positions):
    B = positions.shape[0]
    half = ROPE_DIMS // 2
    inv = ROPE_THETA ** (-jnp.arange(0, ROPE_DIMS, 2, dtype=F32) / ROPE_DIMS)
    table = jax.ShapeDtypeStruct((B, SEQ, LANES), F32)
    out_spec = pl.BlockSpec((1, SEQ, LANES), lambda b: (b, 0, 0))
    return pl.pallas_call(
        _rope_table_kernel,
        grid=(B,),
        in_specs=[pl.BlockSpec((1, 1, SEQ), lambda b: (b, 0, 0)), _resident((half, 1))],
        out_specs=[out_spec] * 3,
        out_shape=[table] * 3,
        compiler_params=_cparams("parallel"),
        name="rope_tables",
    )(positions.reshape(B, 1, SEQ), inv.reshape(half, 1))


def _pool_kernel(u_ref, cnt_ref, w_ref, scale_ref, o_ref):
    row = lax.broadcasted_iota(jnp.int32, (SEQ, LANES), 0)
    low_group = lax.broadcasted_iota(jnp.int32, (SEQ, LANES), 1) < POOL_GROUP

    def shifted(x, s):
        return jnp.where(row >= s, pltpu.roll(x, s, axis=0), 0.0)

    pooled = []
    for c in range(GROUP_W // LANES):
        lanes = slice(c * LANES, (c + 1) * LANES)
        u = u_ref[0, :, lanes]
        lo, hi = POOL_WINDOWS[2 * c], POOL_WINDOWS[2 * c + 1]
        sums, w = {1: u}, 1
        while w < hi:
            sums[2 * w] = sums[w] + shifted(sums[w], w)
            w *= 2
        wsum = jnp.where(low_group, sums[lo], sums[hi])
        pooled.append((wsum / cnt_ref[:, lanes] - u).astype(BF16))
    mixed = jnp.dot(jnp.concatenate(pooled, axis=1), w_ref[...], preferred_element_type=F32)
    o_ref[0] = (mixed * scale_ref[...]).astype(BF16)


def _pool_counts():
    t = np.arange(SEQ)[:, None] + 1
    wnd = np.repeat(np.asarray(POOL_WINDOWS), POOL_GROUP)[None, :]
    return jnp.asarray(np.minimum(t, wnd), F32)


def _pool(u, counts, w_blockdiag, scale):
    B = u.shape[0]
    return pl.pallas_call(
        _pool_kernel,
        grid=(B,),
        in_specs=[
            pl.BlockSpec((1, SEQ, GROUP_W), lambda b: (b, 0, 0)),
            _resident((SEQ, GROUP_W)),
            _resident((GROUP_W, GROUP_W)),
            _resident((1, GROUP_W)),
        ],
        out_specs=pl.BlockSpec((1, SEQ, GROUP_W), lambda b: (b, 0, 0)),
        out_shape=jax.ShapeDtypeStruct((B, SEQ, GROUP_W), BF16),
        compiler_params=_cparams("parallel"),
        name="pool",
    )(u, counts, w_blockdiag, scale)


def _scores_t(k, q):
    return lax.dot_general(k, q, (((1,), (1,)), ((), ())), preferred_element_type=F32)


class _AttnUnit:
    def __init__(self, n_chunks, score_fn, vt_fn, done_fn):
        self.n_chunks, self.score_fn, self.vt_fn, self.done_fn = n_chunks, score_fn, vt_fn, done_fn
        self.m = None
        self.acc = None
        self.scores = {}

    def score(self, j):
        s = self.score_fn(j)
        self.scores[j] = s
        cm = s.max(axis=0, keepdims=True)
        self.m = cm if self.m is None else jnp.maximum(self.m, cm)

    def apply(self, j):
        p = jnp.exp2(self.scores.pop(j) - self.m).astype(BF16)
        pv = jnp.dot(self.vt_fn(j), p, preferred_element_type=F32)
        self.acc = pv if self.acc is None else self.acc + pv

    def finish(self):
        self.done_fn(self.acc[0:HEAD_DIM] / self.acc[HEAD_DIM:HEAD_DIM + 1])


def _pipeline(units):
    prev = None
    for unit in list(units) + [None]:
        n_score = unit.n_chunks if unit is not None else 0
        n_apply = prev.n_chunks if prev is not None else 0
        for j in range(max(n_score, n_apply)):
            if j < n_score:
                unit.score(j)
            if j < n_apply:
                prev.apply(j)
            yield
        if prev is not None:
            prev.finish()
        prev = unit


def _run_interleaved(units, streams=1):
    live = [_pipeline(units[s::streams]) for s in range(streams)]
    while live:
        live = [g for g in live if next(g, StopIteration) is not StopIteration]


def _store_values_t(v_slab_ref_slice, vt_ref, pair):
    vt = v_slab_ref_slice.T
    ones_blk = (lax.broadcasted_iota(jnp.int32, (BF16_ROWS, SEQ), 0) == 0).astype(BF16)
    for hh in range(2):
        h = 2 * pair + hh
        vt_ref[h, 0:HEAD_DIM, :] = vt[hh * HEAD_DIM:(hh + 1) * HEAD_DIM].astype(BF16)
        vt_ref[h, HEAD_DIM:VT_ROWS, :] = ones_blk


def _store_heads(o_ref, r0, outs_t):
    for pair in range(2):
        both = jnp.concatenate(outs_t[2 * pair:2 * pair + 2], axis=0)
        o_ref[0, r0:r0 + Q_BLOCK, pair * LANES:(pair + 1) * LANES] = both.T.astype(BF16)


def _moba_kernel(qkv_ref, causal_ref, o_ref, qa_ref, ka_ref, vt_ref):
    S = SEQ
    blk_row = lax.broadcasted_iota(jnp.int32, (MOBA_NB, S), 0)
    q_blk = lax.broadcasted_iota(jnp.int32, (MOBA_NB, S), 1) // MOBA_BLOCK
    past = blk_row < q_blk
    key_blk = lax.broadcasted_iota(jnp.int32, (S, LANES), 0) // MOBA_BLOCK
    onehot = (lax.broadcasted_iota(jnp.int32, (S, LANES), 1) == key_blk).astype(BF16)
    lane_head = lax.broadcasted_iota(jnp.int32, (1, LANES), 1) // HEAD_DIM

    def split(t):
        hi = t.astype(BF16)
        return hi, (t - hi.astype(F32)).astype(BF16)

    for pair in range(2):
        _store_values_t(qkv_ref[0, :, 2 * GROUP_W + pair * LANES: 2 * GROUP_W + (pair + 1) * LANES], vt_ref, pair)
        q_slab = qkv_ref[0, :, pair * LANES:(pair + 1) * LANES] * Q_SCALE
        k_slab = qkv_ref[0, :, GROUP_W + pair * LANES: GROUP_W + (pair + 1) * LANES]
        kmean = jnp.mean(k_slab.reshape(MOBA_NB, MOBA_BLOCK, LANES), axis=1)
        kmean2 = jnp.concatenate([jnp.where(lane_head == hh, kmean, 0.0) for hh in range(2)], axis=0)
        q_hi, q_lo = split(q_slab)
        k_hi, k_lo = split(kmean2)
        gate2 = _scores_t(jnp.concatenate([k_hi, k_lo, k_hi], axis=1),
                          jnp.concatenate([q_hi, q_hi, q_lo], axis=1))
        for hh in range(2):
            h = 2 * pair + hh
            gate = jnp.where(past, gate2[hh * MOBA_NB:(hh + 1) * MOBA_NB], NEG_INF)
            rank = jnp.zeros((MOBA_NB, S), jnp.int32)
            for jp in range(MOBA_NB):
                gj = gate[jp:jp + 1, :]
                beats = (gj > gate) | ((gj == gate) & (jp < blk_row))
                rank = rank + beats.astype(jnp.int32)
            keep = (past & (rank < MOBA_TOPK)) | (blk_row == q_blk)
            bias_t = jnp.where(keep, 0.0, NEG_INF)
            bias_t = jnp.concatenate([bias_t, jnp.zeros((LANES - MOBA_NB, S), F32)], axis=0)
            qa_ref[h] = jnp.concatenate([q_hi, bias_t.T.astype(BF16)], axis=1)
            ka_ref[h] = jnp.concatenate([jnp.where(lane_head == hh, k_slab, 0.0).astype(BF16), onehot], axis=1)

    def make_unit(i, h, outs):
        r0 = i * Q_BLOCK

        def score(j):
            s = _scores_t(ka_ref[h, j * KEY_CHUNK:(j + 1) * KEY_CHUNK, :], qa_ref[h, r0:r0 + Q_BLOCK, :])
            d = j - i * Q_CHUNKS
            return s + causal_ref[d * KEY_CHUNK:(d + 1) * KEY_CHUNK, :] if d >= 0 else s

        def done(out_t):
            outs[h] = out_t
            if len(outs) == N_HEADS:
                _store_heads(o_ref, r0, [outs[hd] for hd in range(N_HEADS)])

        return _AttnUnit((i + 1) * Q_CHUNKS, score, lambda j: vt_ref[h, :, j * KEY_CHUNK:(j + 1) * KEY_CHUNK], done)

    units = []
    for i in range(SEQ // Q_BLOCK):
        outs = {}
        units += [make_unit(i, h, outs) for h in range(N_HEADS)]
    _run_interleaved(units, MOBA_STREAMS)


def _moba(qkv, causal_bias_t):
    B = qkv.shape[0]
    return pl.pallas_call(
        _moba_kernel,
        grid=(B,),
        in_specs=[
            pl.BlockSpec((1, SEQ, 3 * GROUP_W), lambda b: (b, 0, 0)),
            _resident((Q_BLOCK, Q_BLOCK)),
        ],
        out_specs=pl.BlockSpec((1, SEQ, GROUP_W), lambda b: (b, 0, 0)),
        out_shape=jax.ShapeDtypeStruct((B, SEQ, GROUP_W), BF16),
        scratch_shapes=[
            pltpu.VMEM((N_HEADS, SEQ, 2 * LANES), BF16),
            pltpu.VMEM((N_HEADS, SEQ, 2 * LANES), BF16),
            pltpu.VMEM((N_HEADS, VT_ROWS, SEQ), BF16),
        ],
        compiler_params=_cparams("parallel"),
        name="moba",
    )(qkv, causal_bias_t)


def _dil_kernel(qk_ref, v_ref, bias_ref, o_ref, vt_ref):
    for pair in range(2):
        _store_values_t(v_ref[0, :, pair * LANES:(pair + 1) * LANES], vt_ref, pair)

    def make_unit(i, h, outs):
        r0 = i * Q_BLOCK
        bias0 = SEQ - (r0 + Q_BLOCK)
        q_lanes = slice((h // 2) * LANES, (h // 2 + 1) * LANES)
        k_lanes = slice(GROUP_W + h * LANES, GROUP_W + (h + 1) * LANES)

        def score(j):
            keys = slice(j * KEY_CHUNK, (j + 1) * KEY_CHUNK)
            s = _scores_t(qk_ref[0, keys, k_lanes], qk_ref[0, r0:r0 + Q_BLOCK, q_lanes])
            return s + bias_ref[bias0 + j * KEY_CHUNK: bias0 + (j + 1) * KEY_CHUNK, :]

        def done(out_t):
            outs[h] = out_t
            if len(outs) == N_HEADS:
                _store_heads(o_ref, r0, [outs[hd] for hd in range(N_HEADS)])

        return _AttnUnit((i + 1) * Q_CHUNKS, score, lambda j: vt_ref[h, :, j * KEY_CHUNK:(j + 1) * KEY_CHUNK], done)

    units = []
    for i in range(SEQ // Q_BLOCK):
        outs = {}
        units += [make_unit(i, h, outs) for h in range(N_HEADS)]
    _run_interleaved(units, DIL_STREAMS)


def _dil(qk, v, dist_bias_t):
    B = qk.shape[0]
    return pl.pallas_call(
        _dil_kernel,
        grid=(B,),
        in_specs=[
            pl.BlockSpec((1, SEQ, DIL_QK_W), lambda b: (b, 0, 0)),
            pl.BlockSpec((1, SEQ, GROUP_W), lambda b: (b, 0, 0)),
            _resident((SEQ, Q_BLOCK)),
        ],
        out_specs=pl.BlockSpec((1, SEQ, GROUP_W), lambda b: (b, 0, 0)),
        out_shape=jax.ShapeDtypeStruct((B, SEQ, GROUP_W), BF16),
        scratch_shapes=[pltpu.VMEM((N_HEADS, VT_ROWS, SEQ), BF16)],
        compiler_params=_cparams("parallel"),
        name="dilated",
    )(qk, v, dist_bias_t)


def _causal_bias_t():
    r = np.arange(Q_BLOCK)
    return jnp.asarray(np.where(r[:, None] <= r[None, :], 0.0, NEG_INF), F32)


def _dilated_distance_bias_t():
    r = np.arange(Q_BLOCK)[None, :]
    x = np.arange(SEQ)[:, None]
    d = r - (x - (SEQ - Q_BLOCK))
    cnt = ((d >= 0) & (d <= 128)).astype(np.int64)
    cnt = cnt + ((d >= 0) & (d % 4 == 0) & (d <= 512))
    cnt = cnt + ((d >= 0) & (d % 16 == 0) & (d <= 2048))
    return jnp.asarray(np.where(cnt > 0, np.log2(np.maximum(cnt, 1)), NEG_INF), F32)


def _pool_blockdiag(pool_w):
    n = len(POOL_WINDOWS)
    eye = jnp.eye(n, dtype=pool_w.dtype)
    return jnp.einsum('gcd,gh->gchd', pool_w, eye).reshape(n * POOL_GROUP, n * POOL_GROUP)


def kernel(x, positions, ffn1_norm, ffn1_gate, ffn1_up, ffn1_down, mix_norm, w_in, pool_w, pool_scale,
           conv_w, conv_b, conv_ln_g, conv_ln_b, w_out, ffn2_norm, ffn2_gate, ffn2_up, ffn2_down, final_norm):
    B, S, D = x.shape
    assert (S, D) == (SEQ, D_MODEL)
    rope = [t.reshape(B * S, LANES) for t in _rope_tables(positions)]
    causal_t = _causal_bias_t()
    dist_bias_t = _dilated_distance_bias_t()
    pool_cnt = _pool_counts()
    row = lambda t: t.reshape(1, -1)
    rows = lambda t: t.reshape(DEPTH, 1, -1)
    conv_p = (conv_w, rows(conv_b), rows(conv_ln_g), rows(conv_ln_b))
    pre_mix = (ffn1_gate, ffn1_up, ffn1_down, w_in)
    post_mix = (ffn2_gate, ffn2_up, ffn2_down, w_out)
    wg, wu, wd, wi = (w[0].astype(BF16) for w in pre_mix)
    x2d = x.reshape(B * S, D)
    for l in range(DEPTH):
        x2d, u_pool, qkv_m, qk_d, v_d, u_conv, wg, wu, wd, wo = _ffn_inproj(
            l, x2d, rows(ffn1_norm), wg, wu, wd, rows(mix_norm), wi, *rope, cast=(post_mix, l))
        seq = lambda t: t.reshape(B, S, -1)
        y_pool = _pool(seq(u_pool), pool_cnt, _pool_blockdiag(pool_w[l]).astype(BF16), row(pool_scale[l]))
        y_moba = _moba(seq(qkv_m), causal_t)
        y_dil = _dil(seq(qk_d), seq(v_d), dist_bias_t)
        flat = lambda t: t.reshape(B * S, GROUP_W)
        last = l == DEPTH - 1
        x2d, *nxt = _mix_ffn(l, x2d, flat(y_pool), flat(y_moba), flat(y_dil), u_conv, *conv_p, wo,
                             rows(ffn2_norm), wg, wu, wd, row(final_norm), final_norm=last,
                             cast=None if last else (pre_mix, l + 1))
        if not last:
            wg, wu, wd, wi = nxt
    return x2d.reshape(B, S, D)
```
